```python
import jax, jax.numpy as jnp
from jax import lax
import numpy as np

D_MODEL = 2048
BATCH = 4
SEQ = 2048
DEPTH = 4

GRID_W = 64
CTX_LEN = 256
MIX_W = D_MODEL
SC_W = MIX_W // 2
SC_WIDTH = 3
CF_W = MIX_W - SC_W
CF_KERNEL = 31
EVEN_IN = 3 * SC_W + 2 * CF_W
NA_HEAD_DIM = 128
NA_HEADS = (MIX_W // 2) // NA_HEAD_DIM
NA_W = NA_HEADS * NA_HEAD_DIM
NA_ROWS = 8
NA_COLS = 16
POOL_WINDOWS = (2, 4, 8, 16)
N_POOL = len(POOL_WINDOWS)
POOL_W = MIX_W - NA_W
POOL_GC = POOL_W // N_POOL
ODD_IN = 3 * NA_W + POOL_W
MOE_GROUPS = 4
EXPERTS_PER_GROUP = 8
N_EXPERTS = MOE_GROUPS * EXPERTS_PER_GROUP
MOE_TOP_K = 2
EXPERT_FF = D_MODEL // 8
RMS_EPS = 1e-6
LN_EPS = 1e-5
NEG_INF = -1e30

kernel_name = 'hybrid_flow_conv_natten_pool_hmoe'


def rmsnorm(x, g):
    x32 = x.astype(jnp.float32)
    y = x32 * lax.rsqrt(jnp.mean(x32 * x32, axis=-1, keepdims=True) + RMS_EPS)
    return (y * g.astype(jnp.float32)).astype(x.dtype)


def layernorm(x, g, b):
    x32 = x.astype(jnp.float32)
    mu = jnp.mean(x32, axis=-1, keepdims=True)
    var = jnp.mean(jnp.square(x32 - mu), axis=-1, keepdims=True)
    y = (x32 - mu) * lax.rsqrt(var + LN_EPS)
    return (y * g.astype(jnp.float32) + b.astype(jnp.float32)).astype(x.dtype)


def depthwise_conv(x, w):
    k = w.shape[0]
    return lax.conv_general_dilated(
        x, w[:, None, :].astype(x.dtype), window_strides=(1,),
        padding=[(k // 2, k // 2)], dimension_numbers=('NWC', 'WIO', 'NWC'),
        feature_group_count=x.shape[-1])


def conv_mixers(h, w_in, conv_a, dw_w, dw_b, ln_g, ln_b, w_out):
    u = h @ w_in
    b_gate, c_gate, xa, glu_a, glu_b = jnp.split(
        u, [SC_W, 2 * SC_W, 3 * SC_W, 3 * SC_W + CF_W], axis=-1)
    y_a = b_gate * depthwise_conv(c_gate * xa, conv_a)
    z = glu_a * jax.nn.sigmoid(glu_b)
    z = depthwise_conv(z, dw_w) + dw_b.astype(z.dtype)
    z = jax.nn.silu(layernorm(z, ln_g, ln_b))
    return jnp.concatenate([y_a, z], axis=-1) @ w_out


def multiscale_pool(p, pool_w, pool_scale):
    bsz, length, _ = p.shape
    pg = p.reshape(bsz, length, N_POOL, POOL_GC)
    pg32 = pg.astype(jnp.float32)
    cs = jnp.concatenate([jnp.zeros((bsz, 1, N_POOL, POOL_GC), jnp.float32),
                          jnp.cumsum(pg32, axis=1)], axis=1)
    win = jnp.array(POOL_WINDOWS, dtype=jnp.int32)[None, :]
    t = jnp.arange(length, dtype=jnp.int32)[:, None]
    lo = jnp.maximum(t - win // 2, 0)
    hi = jnp.minimum(t - win // 2 + win - 1, length - 1)
    gidx = jnp.arange(N_POOL)[None, :]
    s = cs[:, hi + 1, gidx] - cs[:, lo, gidx]
    cnt = (hi - lo + 1).astype(jnp.float32)[None, :, :, None]
    mix = (s / cnt - pg32).astype(p.dtype)
    y = jnp.einsum('blgc,gce->blge', mix, pool_w) * pool_scale.reshape(N_POOL, POOL_GC)
    return y.reshape(bsz, length, POOL_W)


def neighbourhood_attention(q, k, v, kc, vc, rpb):
    bsz, length, heads, dh = q.shape
    rows = length // GRID_W
    kr = min(NA_ROWS, rows)
    scale = dh ** -0.5
    qg = q.reshape(bsz, rows, GRID_W, heads, dh)
    kg = k.reshape(bsz, rows, GRID_W, heads, dh)
    vg = v.reshape(bsz, rows, GRID_W, heads, dh)
    r = jnp.arange(rows)
    r_start = jnp.clip(r - kr // 2, 0, rows - kr)
    ridx = r_start[:, None] + jnp.arange(kr)[None, :]
    kw = kg[:, ridx]
    vw = vg[:, ridx]
    s_win = jnp.einsum('brqhd,brjwhd->brhqjw', qg, kw).astype(jnp.float32) * scale
    cols = jnp.arange(GRID_W)
    c_start = jnp.clip(cols - NA_COLS // 2, 0, GRID_W - NA_COLS)
    col_mask = (cols[None, :] >= c_start[:, None]) & (cols[None, :] < c_start[:, None] + NA_COLS)
    dr = ridx - r[:, None] + (NA_ROWS - 1)
    dc = jnp.clip(cols[None, :] - cols[:, None], -(NA_COLS - 1), NA_COLS - 1) + (NA_COLS - 1)
    bias = rpb[:, dr[:, None, :, None], dc[None, :, None, :]]
    s_win = s_win + jnp.transpose(bias, (1, 0, 2, 3, 4)).astype(jnp.float32)[None]
    s_win = jnp.where(col_mask[:, None, :], s_win, NEG_INF)
    s_ctx = jnp.einsum('brqhd,bchd->brhqc', qg, kc).astype(jnp.float32) * scale
    n_win = kr * GRID_W
    s = jnp.concatenate([s_win.reshape(bsz, rows, heads, GRID_W, n_win), s_ctx], axis=-1)
    p = jax.nn.softmax(s, axis=-1).astype(v.dtype)
    p_win = p[..., :n_win].reshape(bsz, rows, heads, GRID_W, kr, GRID_W)
    p_ctx = p[..., n_win:]
    o = (jnp.einsum('brhqjw,brjwhd->brqhd', p_win, vw)
         + jnp.einsum('brhqc,bchd->brqhd', p_ctx, vc))
    return o.reshape(bsz, length, heads * dh)


def context_attention(qc, kc, vc):
    bsz, n, heads, dh = qc.shape
    s = jnp.einsum('bqhd,bkhd->bhqk', qc, kc).astype(jnp.float32) * (dh ** -0.5)
    p = jax.nn.softmax(s, axis=-1).astype(vc.dtype)
    return jnp.einsum('bhqk,bkhd->bqhd', p, vc).reshape(bsz, n, heads * dh)


def attn_pool_mixers(h_lat, h_ctx, w_in, rpb, pool_w, pool_scale, w_out, ctx_out):
    bsz, length, _ = h_lat.shape
    u = h_lat @ w_in
    q, k, v, p = jnp.split(u, [NA_W, 2 * NA_W, 3 * NA_W], axis=-1)
    heads4 = lambda a: a.reshape(a.shape[0], a.shape[1], NA_HEADS, NA_HEAD_DIM)
    if ctx_out:
        qc, kc, vc, pc = jnp.split(h_ctx @ w_in, [NA_W, 2 * NA_W, 3 * NA_W], axis=-1)
    else:
        kc, vc = jnp.split(h_ctx @ w_in[:, NA_W:3 * NA_W], 2, axis=-1)
    kc, vc = heads4(kc), heads4(vc)
    y_na = neighbourhood_attention(heads4(q), heads4(k), heads4(v), kc, vc, rpb)
    y_lat = jnp.concatenate([y_na, multiscale_pool(p, pool_w, pool_scale)], axis=-1) @ w_out
    if not ctx_out:
        return y_lat, None
    y_cna = context_attention(heads4(qc), kc, vc)
    y_ctx = jnp.concatenate([y_cna, multiscale_pool(pc, pool_w, pool_scale)], axis=-1) @ w_out
    return y_lat, y_ctx


def hier_moe(h, rg_w, rg_b, re_w, re_b, w1, w3, w2):
    shape = h.shape
    t = h.reshape(-1, shape[-1])
    n = t.shape[0]
    g_prob = jax.nn.softmax((t @ rg_w).astype(jnp.float32) + rg_b.astype(jnp.float32), axis=-1)
    g_p, g_idx = lax.top_k(g_prob, 1)
    e_logits = ((t @ re_w).astype(jnp.float32) + re_b.astype(jnp.float32)).reshape(
        n, MOE_GROUPS, EXPERTS_PER_GROUP)
    e_logits = jnp.take_along_axis(e_logits, g_idx[:, :, None], axis=1)[:, 0]
    e_p, e_idx = lax.top_k(jax.nn.softmax(e_logits, axis=-1), MOE_TOP_K)
    gate = g_p * e_p / jnp.sum(e_p, axis=-1, keepdims=True)
    expert = g_idx * EXPERTS_PER_GROUP + e_idx
    combine = jnp.sum(jax.nn.one_hot(expert, N_EXPERTS, dtype=jnp.float32) * gate[..., None], axis=1)
    a = jnp.einsum('nd,edf->nef', t, w1)
    b = jnp.einsum('nd,edf->nef', t, w3)
    hid = jax.nn.silu(a) * b * combine[:, :, None].astype(t.dtype)
    return jnp.einsum('nef,efd->nd', hid, w2).reshape(shape)


def setup_inputs(seed: int = 0) -> dict:
    key = jax.random.key(seed)
    ks = iter(jax.random.split(key, 32))

    def nrm(shape, std):
        return jax.random.normal(next(ks), shape, jnp.float32) * std

    d = D_MODEL
    ne = (DEPTH + 1) // 2
    no = DEPTH // 2
    return {
        'x': nrm((BATCH, SEQ, d), 1.0),
        'c': nrm((BATCH, d), 1.0),
        'ctx': nrm((BATCH, CTX_LEN, d), 1.0),
        'c_ctx': nrm((d,), 1.0),
        'ada_w': nrm((DEPTH, d, 6 * d), 0.5 * d ** -0.5),
        'ada_b': nrm((DEPTH, 6 * d), 0.02),
        'norm1_g': 1.0 + nrm((DEPTH, d), 0.02),
        'norm2_g': 1.0 + nrm((DEPTH, d), 0.02),
        'ev_w_in': nrm((ne, d, EVEN_IN), d ** -0.5),
        'ev_conv_a': nrm((ne, SC_WIDTH, SC_W), SC_WIDTH ** -0.5),
        'ev_dw_w': nrm((ne, CF_KERNEL, CF_W), CF_KERNEL ** -0.5),
        'ev_dw_b': nrm((ne, CF_W), 0.02),
        'ev_ln_g': 1.0 + nrm((ne, CF_W), 0.02),
        'ev_ln_b': nrm((ne, CF_W), 0.02),
        'ev_w_out': nrm((ne, MIX_W, d), MIX_W ** -0.5),
        'od_w_in': nrm((no, d, ODD_IN), d ** -0.5),
        'od_rpb': nrm((no, NA_HEADS, 2 * NA_ROWS - 1, 2 * NA_COLS - 1), 0.1),
        'od_pool_w': nrm((no, N_POOL, POOL_GC, POOL_GC), POOL_GC ** -0.5),
        'od_pool_scale': 1.0 + nrm((no, POOL_W), 0.02),
        'od_w_out': nrm((no, MIX_W, d), MIX_W ** -0.5),
        'moe_rg_w': nrm((DEPTH, d, MOE_GROUPS), d ** -0.5),
        'moe_rg_b': nrm((DEPTH, MOE_GROUPS), 0.01),
        'moe_re_w': nrm((DEPTH, d, N_EXPERTS), d ** -0.5),
        'moe_re_b': nrm((DEPTH, N_EXPERTS), 0.01),
        'moe_w1': nrm((DEPTH, N_EXPERTS, d, EXPERT_FF), d ** -0.5),
        'moe_w3': nrm((DEPTH, N_EXPERTS, d, EXPERT_FF), d ** -0.5),
        'moe_w2': nrm((DEPTH, N_EXPERTS, EXPERT_FF, d), EXPERT_FF ** -0.5),
        'final_g': 1.0 + nrm((d,), 0.02),
    }


def reference(x, c, ctx, c_ctx, ada_w, ada_b, norm1_g, norm2_g,
              ev_w_in, ev_conv_a, ev_dw_w, ev_dw_b, ev_ln_g, ev_ln_b, ev_w_out,
              od_w_in, od_rpb, od_pool_w, od_pool_scale, od_w_out,
              moe_rg_w, moe_rg_b, moe_re_w, moe_re_b, moe_w1, moe_w3, moe_w2,
              final_g):
    d = x.shape[-1]
    bsz = x.shape[0]
    x_lat, x_ctx = x, ctx
    for i in range(DEPTH):
        last = i == DEPTH - 1
        j = i // 2
        odd = i % 2 == 1
        need_ctx_in = (not last) or odd
        mod = (jax.nn.silu(c) @ ada_w[i] + ada_b[i]).reshape(bsz, 6, 1, d)
        mod_c = (jax.nn.silu(c_ctx) @ ada_w[i] + ada_b[i]).reshape(6, d)
        h_lat = rmsnorm(x_lat, norm1_g[i]) * (1 + mod[:, 1]) + mod[:, 0]
        h_ctx = rmsnorm(x_ctx, norm1_g[i]) * (1 + mod_c[1]) + mod_c[0] if need_ctx_in else None
        if odd:
            y_lat, y_ctx = attn_pool_mixers(h_lat, h_ctx, od_w_in[j], od_rpb[j], od_pool_w[j],
                                            od_pool_scale[j], od_w_out[j], not last)
        else:
            ev = (ev_w_in[j], ev_conv_a[j], ev_dw_w[j], ev_dw_b[j], ev_ln_g[j], ev_ln_b[j], ev_w_out[j])
            y_lat = conv_mixers(h_lat, *ev)
            y_ctx = None if last else conv_mixers(h_ctx, *ev)
        x_lat = x_lat + mod[:, 2] * y_lat
        if not last:
            x_ctx = x_ctx + mod_c[2] * y_ctx
        moe = (moe_rg_w[i], moe_rg_b[i], moe_re_w[i], moe_re_b[i], moe_w1[i], moe_w3[i], moe_w2[i])
        h2_lat = rmsnorm(x_lat, norm2_g[i]) * (1 + mod[:, 4]) + mod[:, 3]
        if last:
            x_lat = x_lat + mod[:, 5] * hier_moe(h2_lat, *moe)
        else:
            h2_ctx = rmsnorm(x_ctx, norm2_g[i]) * (1 + mod_c[4]) + mod_c[3]
            n_ctx = x_ctx.shape[1]
            f = hier_moe(jnp.concatenate([h2_ctx, h2_lat], axis=1), *moe)
            x_ctx = x_ctx + mod_c[5] * f[:, :n_ctx]
            x_lat = x_lat + mod[:, 5] * f[:, n_ctx:]
    return rmsnorm(x_lat, final_g)
```

```python
import functools

import numpy as np
import jax
import jax.numpy as jnp
from jax import lax
from jax.experimental import pallas as pl
from jax.experimental.pallas import tpu as pltpu

F32 = jnp.float32
BF16 = jnp.bfloat16
U32 = jnp.uint32
I32 = jnp.int32
HIGHEST = lax.Precision.HIGHEST

D = 2048
B = 4
SEQ = 2048
CTX = 256
T = CTX + SEQ
NTOK = B * T
DEPTH = 4
GRID_W = 64
GRID_ROWS = SEQ // GRID_W
HALF = D // 2
SC_W = 1024
CF_KERNEL = 31
SC_WIDTH = 3
NA_HEADS = 8
NA_DH = 128
NA_ROWS = 8
NA_COLS = 16
N_POOL = 4
POOL_GC = 256
N_EXPERTS = 32
N_GROUPS = 4
EPG = 8
EXPERT_FF = 256
RMS_EPS = 1e-6
LN_EPS = 1e-5
NEG_INF = -1e30

ROW_TILE = 256
NSLOT = 2 * NTOK
N_SORT_TILES = NSLOT // ROW_TILE
N_ITEMS = 128
NM_TILE = T // 2
NM_CHUNK = 128
ATT_QROWS = 4
ATT_Q = ATT_QROWS * GRID_W
ATT_KROWS = 12
ATT_K = ATT_KROWS * GRID_W
CONV_CW = 256
CONV_PAD = 16
CONV_ROWS = 32
VMEM_LIMIT = 56 * 1024 * 1024


def _sigmoid(v):
    return 1.0 / (1.0 + jnp.exp(-v))


def _pack_bf16_pair(lo, hi):
    lo_bits = lax.bitcast_convert_type(lo.astype(BF16).astype(F32), U32)
    hi_bits = lax.bitcast_convert_type(hi.astype(BF16).astype(F32), U32)
    return lax.shift_right_logical(lo_bits, jnp.uint32(16)) | (hi_bits & jnp.uint32(0xFFFF0000))


def _unpack_lo(w):
    return lax.bitcast_convert_type(lax.shift_left(w, jnp.uint32(16)), F32)


def _unpack_hi(w):
    return lax.bitcast_convert_type(w & jnp.uint32(0xFFFF0000), F32)


ADA_TN = 1024


def _ada_kernel(c_ref, w_ref, b_ref, o_ref):
    cv = c_ref[...]
    s = cv * _sigmoid(cv)
    o_ref[0] = jnp.dot(s, w_ref[0], preferred_element_type=F32, precision=HIGHEST) + b_ref[0]


def _ada(c8, ada_w, ada_b):
    n = ada_w.shape[-1]
    return pl.pallas_call(
        _ada_kernel,
        grid=(DEPTH, n // ADA_TN),
        in_specs=[
            pl.BlockSpec((8, D), lambda l, j: (0, 0)),
            pl.BlockSpec((1, D, ADA_TN), lambda l, j: (l, 0, j)),
            pl.BlockSpec((1, 1, ADA_TN), lambda l, j: (l, 0, j)),
        ],
        out_specs=pl.BlockSpec((1, 8, ADA_TN), lambda l, j: (l, 0, j)),
        out_shape=jax.ShapeDtypeStruct((DEPTH, 8, n), F32),
        compiler_params=pltpu.CompilerParams(
            dimension_semantics=("arbitrary", "arbitrary"), vmem_limit_bytes=VMEM_LIMIT),
        name="ada",
    )(c8, ada_w, ada_b.reshape(DEPTH, 1, n))


def _norm_mm_kernel(x_ref, modb_ref, modc_ref, g_ref, w_ref, o_ref, h_ref):
    m = pl.program_id(1)
    n = pl.program_id(2)

    @pl.when(n == 0)
    def _():
        for c in range(NM_TILE // NM_CHUNK):
            xv = x_ref[pl.ds(c * NM_CHUNK, NM_CHUNK), :]
            ms = jnp.mean(xv * xv, axis=-1, keepdims=True)
            y = xv * lax.rsqrt(ms + RMS_EPS) * g_ref[...]
            rows = m * NM_TILE + c * NM_CHUNK + lax.broadcasted_iota(I32, (NM_CHUNK, 1), 0)
            is_ctx = rows < CTX
            shift = jnp.where(is_ctx, modc_ref[0:1, :], modb_ref[0:1, :])
            scale = jnp.where(is_ctx, modc_ref[1:2, :], modb_ref[1:2, :])
            h_ref[pl.ds(c * NM_CHUNK, NM_CHUNK), :] = (y * (1.0 + scale) + shift).astype(BF16)

    o_ref[...] = jnp.dot(h_ref[...], w_ref[...].astype(BF16),
                         preferred_element_type=F32).astype(BF16)


def _norm_mm(xc, mods, layer, g, w_all, widx, tn):
    n_out = w_all.shape[-1]
    return pl.pallas_call(
        _norm_mm_kernel,
        grid=(B, T // NM_TILE, n_out // tn),
        in_specs=[
            pl.BlockSpec((None, NM_TILE, D), lambda b, m, n: (b, m, 0)),
            pl.BlockSpec((None, None, 6, D), lambda b, m, n: (layer, b, 0, 0)),
            pl.BlockSpec((None, None, 6, D), lambda b, m, n: (layer, B, 0, 0)),
            pl.BlockSpec((None, 1, D), lambda b, m, n: (layer, 0, 0)),
            pl.BlockSpec((None, D, tn), lambda b, m, n: (widx, 0, n)),
        ],
        out_specs=pl.BlockSpec((None, NM_TILE, tn), lambda b, m, n: (b, m, n)),
        out_shape=jax.ShapeDtypeStruct((B, T, n_out), BF16),
        scratch_shapes=[pltpu.VMEM((NM_TILE, D), BF16)],
        compiler_params=pltpu.CompilerParams(
            dimension_semantics=("arbitrary", "arbitrary", "arbitrary"),
            vmem_limit_bytes=VMEM_LIMIT),
        name="norm_mm",
    )(xc, mods, mods, g, w_all)


def _conv_kernel(bg_ref, cg_ref, xa_ref, ga_ref, gb_ref, ca_ref, dw_ref, dwb_ref,
                 ya_ref, zc_ref, vs_ref, zs_ref):
    ca = ca_ref[...]
    dw = dw_ref[...]
    dwb = dwb_ref[...]
    zeros_pad = jnp.zeros((CONV_PAD, CONV_CW), F32)

    def run_sequence(seq0, length):
        vs_ref[pl.ds(0, CONV_PAD), :] = zeros_pad
        zs_ref[pl.ds(0, CONV_PAD), :] = zeros_pad
        vs_ref[pl.ds(CONV_PAD + length, CONV_PAD), :] = zeros_pad
        zs_ref[pl.ds(CONV_PAD + length, CONV_PAD), :] = zeros_pad

        def fill(i, carry):
            r = pl.multiple_of(i * ROW_TILE, ROW_TILE)
            src = pl.ds(seq0 + r, ROW_TILE)
            dst = pl.ds(CONV_PAD + r, ROW_TILE)
            vs_ref[dst, :] = cg_ref[src, :].astype(F32) * xa_ref[src, :].astype(F32)
            zs_ref[dst, :] = ga_ref[src, :].astype(F32) * _sigmoid(gb_ref[src, :].astype(F32))
            return carry

        lax.fori_loop(0, length // ROW_TILE, fill, 0)

        def conv(i, carry):
            r = pl.multiple_of(i * CONV_ROWS, CONV_ROWS)
            win = pl.ds(r, CONV_ROWS + 2 * CONV_PAD)
            vwin = vs_ref[win, :]
            acc3 = jnp.zeros((CONV_ROWS, CONV_CW), F32)
            for k in range(SC_WIDTH):
                o = CONV_PAD + k - SC_WIDTH // 2
                acc3 = acc3 + vwin[o:o + CONV_ROWS, :] * ca[k:k + 1, :]
            out = pl.ds(seq0 + r, CONV_ROWS)
            ya_ref[out, :] = (bg_ref[out, :].astype(F32) * acc3).astype(BF16)
            zwin = zs_ref[win, :]
            acc = jnp.zeros((CONV_ROWS, CONV_CW), F32)
            for k in range(CF_KERNEL):
                o = CONV_PAD + k - CF_KERNEL // 2
                acc = acc + zwin[o:o + CONV_ROWS, :] * dw[k:k + 1, :]
            zc_ref[out, :] = (acc + dwb).astype(BF16)
            return carry

        lax.fori_loop(0, length // CONV_ROWS, conv, 0)

    run_sequence(0, CTX)
    run_sequence(CTX, SEQ)


def _conv(u, conv_a, dw_w, dw_b, j):
    nblk = SC_W // CONV_CW
    part = lambda k: pl.BlockSpec((None, T, CONV_CW), lambda b, cb: (b, 0, k * nblk + cb))
    out_spec = pl.BlockSpec((None, T, CONV_CW), lambda b, cb: (b, 0, cb))
    return pl.pallas_call(
        _conv_kernel,
        grid=(B, nblk),
        in_specs=[part(0), part(1), part(2), part(3), part(4),
                  pl.BlockSpec((None, SC_WIDTH, CONV_CW), lambda b, cb: (j, 0, cb)),
                  pl.BlockSpec((None, CF_KERNEL, CONV_CW), lambda b, cb: (j, 0, cb)),
                  pl.BlockSpec((None, 1, CONV_CW), lambda b, cb: (j, 0, cb))],
        out_specs=[out_spec, out_spec],
        out_shape=[jax.ShapeDtypeStruct((B, T, HALF), BF16)] * 2,
        scratch_shapes=[pltpu.VMEM((SEQ + 2 * CONV_PAD, CONV_CW), F32)] * 2,
        compiler_params=pltpu.CompilerParams(
            dimension_semantics=("arbitrary", "arbitrary"), vmem_limit_bytes=VMEM_LIMIT),
        name="conv_mixers",
    )(u, u, u, u, u, conv_a, dw_w, dw_b.reshape(dw_b.shape[0], 1, dw_b.shape[1]))


ATT_VARIANTS = ((0, 0), (ATT_QROWS, 0), (GRID_ROWS - ATT_QROWS, GRID_ROWS - ATT_KROWS))


def _att_block_geometry(blk):
    r0 = blk * ATT_QROWS
    if blk == 0:
        return 0, 0
    if blk == GRID_ROWS // ATT_QROWS - 1:
        return 2, GRID_ROWS - ATT_KROWS
    return 1, r0 - NA_ROWS // 2


def _softmax_pv(parts):
    mx = None
    for s, _ in parts:
        pm = jnp.max(s, axis=-1, keepdims=True)
        mx = pm if mx is None else jnp.maximum(mx, pm)
    den = None
    num = None
    for s, v in parts:
        e = jnp.exp(s - mx)
        ps = jnp.sum(e, axis=-1, keepdims=True)
        pv = jnp.dot(e.astype(BF16), v, preferred_element_type=F32)
        den = ps if den is None else den + ps
        num = pv if num is None else num + pv
    return num / den


def _att_kernel(q_ref, k_ref, v_ref, bias_ref, o_ref):
    scale = NA_DH ** -0.5
    nt = (((1,), (1,)), ((), ()))
    kc = k_ref[pl.ds(0, CTX), :]
    vc = v_ref[pl.ds(0, CTX), :]

    qc = q_ref[pl.ds(0, CTX), :]
    s_cc = lax.dot_general(qc, kc, nt, preferred_element_type=F32) * scale
    o_ref[pl.ds(0, CTX), :] = _softmax_pv([(s_cc, vc)]).astype(BF16)

    for blk in range(GRID_ROWS // ATT_QROWS):
        variant, kr0 = _att_block_geometry(blk)
        qrows = pl.ds(CTX + blk * ATT_Q, ATT_Q)
        krows = pl.ds(CTX + kr0 * GRID_W, ATT_K)
        qb = q_ref[qrows, :]
        s_w = lax.dot_general(qb, k_ref[krows, :], nt, preferred_element_type=F32) * scale
        s_w = s_w + bias_ref[variant]
        s_c = lax.dot_general(qb, kc, nt, preferred_element_type=F32) * scale
        o_ref[qrows, :] = _softmax_pv([(s_w, v_ref[krows, :]), (s_c, vc)]).astype(BF16)


def _att_bias_table(rpb):
    qr = np.arange(ATT_Q) // GRID_W
    qc = np.arange(ATT_Q) % GRID_W
    kj = np.arange(ATT_K) // GRID_W
    kw = np.arange(ATT_K) % GRID_W
    drs, dcs, valids = [], [], []
    for r0, kr0 in ATT_VARIANTS:
        r = (r0 + qr)[:, None]
        kr = (kr0 + kj)[None, :]
        rs = np.clip(r - NA_ROWS // 2, 0, GRID_ROWS - NA_ROWS)
        vrow = (kr >= rs) & (kr < rs + NA_ROWS)
        dr = np.clip(kr - r + (NA_ROWS - 1), 0, 2 * NA_ROWS - 2)
        dc = np.clip(kw[None, :] - qc[:, None], -(NA_COLS - 1), NA_COLS - 1) + (NA_COLS - 1)
        cs = np.clip(qc - NA_COLS // 2, 0, GRID_W - NA_COLS)[:, None]
        vcol = (kw[None, :] >= cs) & (kw[None, :] < cs + NA_COLS)
        drs.append(dr + 0 * dc)
        dcs.append(dc + 0 * dr)
        valids.append(vrow & vcol)
    dr = np.stack(drs)
    dc = np.stack(dcs)
    valid = np.stack(valids)
    vals = rpb[:, dr, dc]
    return jnp.where(valid[None], vals, NEG_INF).astype(F32)


def _attention(u, bias):
    col = lambda off: pl.BlockSpec((None, T, NA_DH), lambda h, b: (b, 0, off + h))
    return pl.pallas_call(
        _att_kernel,
        grid=(NA_HEADS, B),
        in_specs=[col(0), col(NA_HEADS), col(2 * NA_HEADS),
                  pl.BlockSpec((None, 3, ATT_Q, ATT_K), lambda h, b: (h, 0, 0, 0))],
        out_specs=pl.BlockSpec((None, T, NA_DH), lambda h, b: (b, 0, h)),
        out_shape=jax.ShapeDtypeStruct((B, T, HALF), BF16),
        compiler_params=pltpu.CompilerParams(
            dimension_semantics=("arbitrary", "arbitrary"), vmem_limit_bytes=VMEM_LIMIT),
        name="neighbourhood_attention",
    )(u, u, u, bias)


POOL_KWIN = 512


def _pool_kernel(p_ref, pw_ref, ps_ref, o_ref):
    half = lax.shift_left(jnp.int32(1), pl.program_id(1))
    pw = pw_ref[...].astype(BF16)
    sc = ps_ref[...]

    def tile(row0, seq0, length, ks, kwin):
        t = (row0 - seq0) + lax.broadcasted_iota(I32, (ROW_TILE, 1), 0)
        s = ks + lax.broadcasted_iota(I32, (1, kwin), 1)
        lo = t - half
        hi = t + half - 1
        band = jnp.where((s >= lo) & (s <= hi), 1.0, 0.0).astype(BF16)
        ssum = jnp.dot(band, p_ref[pl.ds(seq0 + ks, kwin), :], preferred_element_type=F32)
        cnt = (jnp.minimum(hi, length - 1) - jnp.maximum(lo, 0) + 1).astype(F32)
        mix = ssum / cnt - p_ref[pl.ds(row0, ROW_TILE), :].astype(F32)
        y = jnp.dot(mix.astype(BF16), pw, preferred_element_type=F32) * sc
        o_ref[pl.ds(row0, ROW_TILE), :] = y.astype(BF16)

    tile(0, 0, CTX, 0, CTX)
    for jt in range(SEQ // ROW_TILE):
        ks = min(max(ROW_TILE * jt - (POOL_KWIN - ROW_TILE) // 2, 0), SEQ - POOL_KWIN)
        tile(CTX + ROW_TILE * jt, CTX, SEQ, ks, POOL_KWIN)


def _pool(u, pool_w, pool_scale, j):
    p_block0 = 3 * HALF // POOL_GC
    return pl.pallas_call(
        _pool_kernel,
        grid=(B, N_POOL),
        in_specs=[pl.BlockSpec((None, T, POOL_GC), lambda b, g: (b, 0, p_block0 + g)),
                  pl.BlockSpec((None, None, POOL_GC, POOL_GC), lambda b, g: (j, g, 0, 0)),
                  pl.BlockSpec((None, None, 1, POOL_GC), lambda b, g: (j, g, 0, 0))],
        out_specs=pl.BlockSpec((None, T, POOL_GC), lambda b, g: (b, 0, g)),
        out_shape=jax.ShapeDtypeStruct((B, T, HALF), BF16),
        compiler_params=pltpu.CompilerParams(
            dimension_semantics=("arbitrary", "arbitrary"), vmem_limit_bytes=VMEM_LIMIT),
        name="pool_mixer",
    )(u, pool_w, pool_scale.reshape(pool_scale.shape[0], N_POOL, 1, POOL_GC))


ROUTE_LANES = 128


def _route(logits):
    lane = lax.broadcasted_iota(I32, logits.shape, 1)
    lanef = lane.astype(F32)
    is_g = lane < N_GROUPS
    gl = jnp.where(is_g, logits, NEG_INF)
    ge = jnp.where(is_g, jnp.exp(gl - jnp.max(gl, axis=-1, keepdims=True)), 0.0)
    gp = ge / jnp.sum(ge, axis=-1, keepdims=True)
    g_p = jnp.max(gp, axis=-1, keepdims=True)
    g_idx = jnp.min(jnp.where(is_g & (gp == g_p), lanef, float(ROUTE_LANES)), axis=-1, keepdims=True)
    first = float(N_GROUPS) + float(EPG) * g_idx
    sel = (lanef >= first) & (lanef < first + float(EPG))
    el = jnp.where(sel, logits, NEG_INF)
    ee = jnp.where(sel, jnp.exp(el - jnp.max(el, axis=-1, keepdims=True)), 0.0)
    ep = jnp.where(sel, ee / jnp.sum(ee, axis=-1, keepdims=True), -1.0)
    p1 = jnp.max(ep, axis=-1, keepdims=True)
    i1 = jnp.min(jnp.where(ep == p1, lanef, float(ROUTE_LANES)), axis=-1, keepdims=True)
    ep2 = jnp.where(lanef == i1, -1.0, ep)
    p2 = jnp.max(ep2, axis=-1, keepdims=True)
    i2 = jnp.min(jnp.where(ep2 == p2, lanef, float(ROUTE_LANES)), axis=-1, keepdims=True)
    psum = p1 + p2
    gate1 = g_p * p1 / psum
    gate2 = g_p * p2 / psum
    return jnp.where(lane == 0, i1 - float(N_GROUPS),
                     jnp.where(lane == 1, i2 - float(N_GROUPS),
                               jnp.where(lane == 2, gate1,
                                         jnp.where(lane == 3, gate2, 0.0))))


def _proj_kernel(ya_ref, yb_ref, w_ref, lng_ref, lnb_ref, x_ref, modb_ref, modc_ref, g2_ref,
                 rw_ref, rb_ref, xo_ref, hp_ref, ri_ref, *, even):
    ya = ya_ref[...]
    if even:
        zc = yb_ref[...].astype(F32)
        mu = jnp.mean(zc, axis=-1, keepdims=True)
        dz = zc - mu
        var = jnp.mean(dz * dz, axis=-1, keepdims=True)
        z = dz * lax.rsqrt(var + LN_EPS) * lng_ref[...] + lnb_ref[...]
        yb = (z * _sigmoid(z)).astype(BF16)
    else:
        yb = yb_ref[...]
    out = (jnp.dot(ya, w_ref[pl.ds(0, HALF), :], preferred_element_type=F32)
           + jnp.dot(yb, w_ref[pl.ds(HALF, HALF), :], preferred_element_type=F32))
    mod = jnp.where(pl.program_id(1) == 0, modc_ref[...], modb_ref[...])
    xn = x_ref[...] + mod[2:3, :] * out
    xo_ref[...] = xn
    ms = jnp.mean(xn * xn, axis=-1, keepdims=True)
    h2 = xn * lax.rsqrt(ms + RMS_EPS) * g2_ref[...]
    h2 = h2 * (1.0 + mod[4:5, :]) + mod[3:4, :]
    hp_ref[...] = _pack_bf16_pair(h2[:, :HALF], h2[:, HALF:])
    logits = jnp.dot(h2, rw_ref[...], preferred_element_type=F32, precision=HIGHEST) + rb_ref[...]
    ri_ref[...] = _route(logits)


def _proj(ya, yb, w_bf, widx, ln_g, ln_b, xc, mods, layer, g2, rw, rb, even):
    row = lambda width: pl.BlockSpec((None, ROW_TILE, width), lambda b, m: (b, m, 0))
    return pl.pallas_call(
        functools.partial(_proj_kernel, even=even),
        grid=(B, T // ROW_TILE),
        in_specs=[row(HALF), row(HALF),
                  pl.BlockSpec((None, D, D), lambda b, m: (widx, 0, 0)),
                  pl.BlockSpec((None, 1, HALF), lambda b, m: (widx, 0, 0)),
                  pl.BlockSpec((None, 1, HALF), lambda b, m: (widx, 0, 0)),
                  row(D),
                  pl.BlockSpec((None, None, 6, D), lambda b, m: (layer, b, 0, 0)),
                  pl.BlockSpec((None, None, 6, D), lambda b, m: (layer, B, 0, 0)),
                  pl.BlockSpec((None, 1, D), lambda b, m: (layer, 0, 0)),
                  pl.BlockSpec((None, D, ROUTE_LANES), lambda b, m: (layer, 0, 0)),
                  pl.BlockSpec((None, 1, ROUTE_LANES), lambda b, m: (layer, 0, 0))],
        out_specs=[row(D), row(HALF), row(ROUTE_LANES)],
        out_shape=[jax.ShapeDtypeStruct((B, T, D), F32),
                   jax.ShapeDtypeStruct((B, T, HALF), U32),
                   jax.ShapeDtypeStruct((B, T, ROUTE_LANES), F32)],
        compiler_params=pltpu.CompilerParams(
            dimension_semantics=("arbitrary", "arbitrary"), vmem_limit_bytes=VMEM_LIMIT),
        name="out_proj_router",
    )(ya, yb, w_bf, ln_g, ln_b, xc, mods, mods, g2, rw, rb)


def _positions_kernel(ri_ref, pos_ref, base_ref):
    ntiles = NTOK // ROW_TILE
    lanef = lax.broadcasted_iota(I32, (ROW_TILE, ROUTE_LANES), 1).astype(F32)

    def count(i, acc):
        r = ri_ref[pl.ds(pl.multiple_of(i * ROW_TILE, ROW_TILE), ROW_TILE), :]
        hit = (lanef == r[:, 0:1]) | (lanef == r[:, 1:2])
        return acc + jnp.sum(jnp.where(hit, 1.0, 0.0), axis=0, keepdims=True)

    cnt = lax.fori_loop(0, ntiles, count, jnp.zeros((1, ROUTE_LANES), F32))
    ri = lax.broadcasted_iota(I32, (ROUTE_LANES, ROUTE_LANES), 0)
    ci = lax.broadcasted_iota(I32, (ROUTE_LANES, ROUTE_LANES), 1)
    before = jnp.where(ri < ci, 1.0, 0.0)
    cnt8 = jnp.broadcast_to(cnt, (8, ROUTE_LANES))
    base8 = jnp.dot(cnt8, before, preferred_element_type=F32, precision=HIGHEST)
    base_ref[...] = base8.astype(I32)
    base_col = jnp.broadcast_to(base8[0:1, :], (ROUTE_LANES, ROUTE_LANES)).T[:, 0:1]

    ti = lax.broadcasted_iota(I32, (ROW_TILE, ROW_TILE), 0)
    tj = lax.broadcasted_iota(I32, (ROW_TILE, ROW_TILE), 1)
    earlier = jnp.where(ti < tj, 1.0, 0.0).astype(BF16)
    subf = lax.broadcasted_iota(I32, (ROUTE_LANES, ROW_TILE), 0).astype(F32)

    def place(i, carry):
        start = pl.multiple_of(i * ROW_TILE, ROW_TILE)
        rt = ri_ref[pl.ds(start, ROW_TILE), :].T
        e0 = rt[0:1, :]
        e1 = rt[1:2, :]
        hit0 = subf == e0
        hit1 = subf == e1
        hits = jnp.where(hit0 | hit1, 1.0, 0.0)
        rank = jnp.dot(hits.astype(BF16), earlier, preferred_element_type=F32) + carry + base_col
        pos_ref[0:1, pl.ds(start, ROW_TILE)] = jnp.sum(
            jnp.where(hit0, rank, 0.0), axis=0, keepdims=True).astype(I32)
        pos_ref[1:2, pl.ds(start, ROW_TILE)] = jnp.sum(
            jnp.where(hit1, rank, 0.0), axis=0, keepdims=True).astype(I32)
        return carry + jnp.sum(hits, axis=1, keepdims=True)

    lax.fori_loop(0, ntiles, place, jnp.zeros((ROUTE_LANES, 1), F32))


def _positions(ri_flat):
    return pl.pallas_call(
        _positions_kernel,
        out_shape=[jax.ShapeDtypeStruct((2, NTOK), I32),
                   jax.ShapeDtypeStruct((8, ROUTE_LANES), I32)],
        compiler_params=pltpu.CompilerParams(vmem_limit_bytes=VMEM_LIMIT),
        name="moe_positions",
    )(ri_flat)


def _items_kernel(base_ref, it_ref):
    def per_expert(e, n):
        s = base_ref[e]
        t = base_ref[e + 1]
        first = lax.shift_right_logical(s, 8)
        last = lax.shift_right_logical(jnp.maximum(t - 1, 0), 8)
        ntile = jnp.where(t > s, last - first + 1, 0)

        def per_tile(k, n2):
            tile = first + k
            it_ref[n2] = tile
            it_ref[N_ITEMS + n2] = e
            it_ref[2 * N_ITEMS + n2] = jnp.maximum(s - tile * ROW_TILE, 0)
            it_ref[3 * N_ITEMS + n2] = jnp.minimum(t - tile * ROW_TILE, ROW_TILE)
            return n2 + 1

        return lax.fori_loop(0, ntile, per_tile, n)

    n_used = lax.fori_loop(0, N_EXPERTS, per_expert, jnp.int32(0))
    last_e = it_ref[N_ITEMS + n_used - 1]

    def pad(i, carry):
        it_ref[i] = N_SORT_TILES - 1
        it_ref[N_ITEMS + i] = last_e
        it_ref[2 * N_ITEMS + i] = 0
        it_ref[3 * N_ITEMS + i] = 0
        return carry

    lax.fori_loop(n_used, N_ITEMS, pad, 0)


def _items(base_flat):
    return pl.pallas_call(
        _items_kernel,
        in_specs=[pl.BlockSpec(memory_space=pltpu.SMEM)],
        out_specs=pl.BlockSpec(memory_space=pltpu.SMEM),
        out_shape=jax.ShapeDtypeStruct((4 * N_ITEMS,), I32),
        name="moe_items",
    )(base_flat)


def _row_copy(src_hbm, src_row, dst_hbm, dst_row, sem):
    return pltpu.make_async_copy(src_hbm.at[pl.ds(src_row, 1)], dst_hbm.at[pl.ds(dst_row, 1)], sem)


def _scatter_kernel(pos_ref, h_hbm, xs_hbm, sem):
    nchunk = NTOK // ROW_TILE

    def wait_chunk():
        pltpu.make_async_copy(h_hbm.at[pl.ds(0, 2 * ROW_TILE)],
                              xs_hbm.at[pl.ds(0, 2 * ROW_TILE)], sem).wait()

    def chunk(c, carry):
        def token(r, carry2):
            t = c * ROW_TILE + r
            _row_copy(h_hbm, t, xs_hbm, pos_ref[t], sem).start()
            _row_copy(h_hbm, t, xs_hbm, pos_ref[NTOK + t], sem).start()
            return carry2

        lax.fori_loop(0, ROW_TILE, token, 0)

        @pl.when(c > 0)
        def _():
            wait_chunk()

        return carry

    lax.fori_loop(0, nchunk, chunk, 0)
    wait_chunk()


def _scatter(pos_flat, hp_flat):
    return pl.pallas_call(
        _scatter_kernel,
        grid_spec=pltpu.PrefetchScalarGridSpec(
            num_scalar_prefetch=1,
            grid=(1,),
            in_specs=[pl.BlockSpec(memory_space=pl.ANY)],
            out_specs=pl.BlockSpec(memory_space=pl.ANY),
            scratch_shapes=[pltpu.SemaphoreType.DMA(())],
        ),
        out_shape=jax.ShapeDtypeStruct((NSLOT, HALF), U32),
        compiler_params=pltpu.CompilerParams(
            dimension_semantics=("arbitrary",), has_side_effects=True),
        name="moe_scatter",
    )(pos_flat, hp_flat)


def _gmm_kernel(it_ref, xs_ref, w1_ref, w3_ref, w2_ref, o_ref, acc_ref):
    i = pl.program_id(0)
    tile = it_ref[i]
    lo = it_ref[2 * N_ITEMS + i]
    hi = it_ref[3 * N_ITEMS + i]
    prev_tile = it_ref[jnp.maximum(i - 1, 0)]

    @pl.when((i == 0) | (tile != prev_tile))
    def _():
        acc_ref[...] = jnp.zeros_like(acc_ref)

    @pl.when(hi > lo)
    def _():
        xw = xs_ref[...]
        x_lo = _unpack_lo(xw).astype(BF16)
        x_hi = _unpack_hi(xw).astype(BF16)

        def up(w_ref):
            return (jnp.dot(x_lo, w_ref[pl.ds(0, HALF), :].astype(BF16), preferred_element_type=F32)
                    + jnp.dot(x_hi, w_ref[pl.ds(HALF, HALF), :].astype(BF16),
                              preferred_element_type=F32))

        a = up(w1_ref)
        b = up(w3_ref)
        rows = lax.broadcasted_iota(I32, (ROW_TILE, 1), 0)
        hid = jnp.where((rows >= lo) & (rows < hi), a * _sigmoid(a) * b, 0.0)
        acc_ref[...] += jnp.dot(hid.astype(BF16), w2_ref[...].astype(BF16),
                                preferred_element_type=F32)
        acc = acc_ref[...]
        o_ref[...] = _pack_bf16_pair(acc[:, :HALF], acc[:, HALF:])


def _gmm(items, xs, w1, w3, w2, layer):
    return pl.pallas_call(
        _gmm_kernel,
        grid_spec=pltpu.PrefetchScalarGridSpec(
            num_scalar_prefetch=1,
            grid=(N_ITEMS,),
            in_specs=[
                pl.BlockSpec((ROW_TILE, HALF), lambda i, it: (it[i], 0)),
                pl.BlockSpec((None, None, D, EXPERT_FF), lambda i, it: (layer, it[N_ITEMS + i], 0, 0)),
                pl.BlockSpec((None, None, D, EXPERT_FF), lambda i, it: (layer, it[N_ITEMS + i], 0, 0)),
                pl.BlockSpec((None, None, EXPERT_FF, D), lambda i, it: (layer, it[N_ITEMS + i], 0, 0)),
            ],
            out_specs=pl.BlockSpec((ROW_TILE, HALF), lambda i, it: (it[i], 0)),
            scratch_shapes=[pltpu.VMEM((ROW_TILE, D), F32)],
        ),
        out_shape=jax.ShapeDtypeStruct((NSLOT, HALF), U32),
        compiler_params=pltpu.CompilerParams(
            dimension_semantics=("arbitrary",), vmem_limit_bytes=VMEM_LIMIT),
        name="moe_experts",
    )(items, xs, w1, w3, w2)


def _combine_kernel(pos_ref, ys_hbm, x_ref, ri_ref, modb_ref, modc_ref, fg_ref, o_ref, buf, sems,
                    *, final, tiles_per_b, m_off):
    i = pl.program_id(0)
    ntiles = pl.num_programs(0)

    def issue(tile_i, par):
        t0 = (tile_i // tiles_per_b) * T + (tile_i % tiles_per_b + m_off) * ROW_TILE

        def row(r, carry):
            for s in range(2):
                pltpu.make_async_copy(ys_hbm.at[pl.ds(pos_ref[s * NTOK + t0 + r], 1)],
                                      buf.at[par, s, pl.ds(r, 1)], sems.at[par]).start()
            return carry

        lax.fori_loop(0, ROW_TILE, row, 0)

    @pl.when(i == 0)
    def _():
        issue(0, 0)

    @pl.when(i + 1 < ntiles)
    def _():
        issue(i + 1, (i + 1) % 2)

    par = i % 2
    pltpu.make_async_copy(buf.at[par], buf.at[par], sems.at[par]).wait()
    w0 = buf[par, 0]
    w1 = buf[par, 1]
    ri = ri_ref[...]
    g0 = ri[:, 2:3]
    g1 = ri[:, 3:4]
    m = i % tiles_per_b + m_off
    mod = jnp.where(m == 0, modc_ref[...], modb_ref[...])
    gate = mod[5:6, :]
    xv = x_ref[...]
    lo = xv[:, :HALF] + gate[:, :HALF] * (g0 * _unpack_lo(w0) + g1 * _unpack_lo(w1))
    hi = xv[:, HALF:] + gate[:, HALF:] * (g0 * _unpack_hi(w0) + g1 * _unpack_hi(w1))
    if final:
        ms = (jnp.sum(lo * lo, axis=-1, keepdims=True)
              + jnp.sum(hi * hi, axis=-1, keepdims=True)) / float(D)
        inv = lax.rsqrt(ms + RMS_EPS)
        fg = fg_ref[...]
        lo = lo * inv * fg[:, :HALF]
        hi = hi * inv * fg[:, HALF:]
    o_ref[:, pl.ds(0, HALF)] = lo
    o_ref[:, pl.ds(HALF, HALF)] = hi


def _combine(pos_flat, ys, xc, ri, mods, layer, final_g, final):
    if final:
        tiles_per_b, m_off, rows_out = SEQ // ROW_TILE, CTX // ROW_TILE, SEQ
    else:
        tiles_per_b, m_off, rows_out = T // ROW_TILE, 0, T
    tok = lambda width: pl.BlockSpec(
        (None, ROW_TILE, width), lambda i, pos: (i // tiles_per_b, i % tiles_per_b + m_off, 0))
    return pl.pallas_call(
        functools.partial(_combine_kernel, final=final, tiles_per_b=tiles_per_b, m_off=m_off),
        grid_spec=pltpu.PrefetchScalarGridSpec(
            num_scalar_prefetch=1,
            grid=(B * tiles_per_b,),
            in_specs=[
                pl.BlockSpec(memory_space=pl.ANY),
                tok(D),
                tok(ROUTE_LANES),
                pl.BlockSpec((None, None, 6, D), lambda i, pos: (layer, i // tiles_per_b, 0, 0)),
                pl.BlockSpec((None, None, 6, D), lambda i, pos: (layer, B, 0, 0)),
                pl.BlockSpec((1, D), lambda i, pos: (0, 0)),
            ],
            out_specs=pl.BlockSpec((None, ROW_TILE, D),
                                   lambda i, pos: (i // tiles_per_b, i % tiles_per_b, 0)),
            scratch_shapes=[pltpu.VMEM((2, 2, ROW_TILE, HALF), U32),
                            pltpu.SemaphoreType.DMA((2,))],
        ),
        out_shape=jax.ShapeDtypeStruct((B, rows_out, D), F32),
        compiler_params=pltpu.CompilerParams(
            dimension_semantics=("arbitrary",), vmem_limit_bytes=VMEM_LIMIT),
        name="moe_combine",
    )(pos_flat, ys, xc, ri, mods, mods, final_g)


def kernel(x, c, ctx, c_ctx, ada_w, ada_b, norm1_g, norm2_g, ev_w_in, ev_conv_a, ev_dw_w, ev_dw_b,
           ev_ln_g, ev_ln_b, ev_w_out, od_w_in, od_rpb, od_pool_w, od_pool_scale, od_w_out,
           moe_rg_w, moe_rg_b, moe_re_w, moe_re_b, moe_w1, moe_w3, moe_w2, final_g):
    xc = jnp.concatenate([ctx, x], axis=1)
    c8 = jnp.concatenate([c, c_ctx[None, :], jnp.zeros((8 - B - 1, D), F32)], axis=0)
    mods = _ada(c8, ada_w, ada_b).reshape(DEPTH, 8, 6, D)

    ev_w_out_bf = ev_w_out.astype(BF16)
    od_w_out_bf = od_w_out.astype(BF16)
    pad = ROUTE_LANES - N_GROUPS - N_EXPERTS
    rw = jnp.concatenate([moe_rg_w, moe_re_w, jnp.zeros((DEPTH, D, pad), F32)], axis=-1)
    rb = jnp.concatenate([moe_rg_b, moe_re_b, jnp.zeros((DEPTH, pad), F32)],
                         axis=-1).reshape(DEPTH, 1, ROUTE_LANES)
    final_g2 = final_g.reshape(1, D)
    norm1_g = norm1_g.reshape(DEPTH, 1, D)
    norm2_g = norm2_g.reshape(DEPTH, 1, D)
    ev_ln_g = ev_ln_g.reshape(-1, 1, HALF)
    ev_ln_b = ev_ln_b.reshape(-1, 1, HALF)

    for i in range(DEPTH):
        j = i // 2
        last = i == DEPTH - 1
        if i % 2 == 1:
            u = _norm_mm(xc, mods, i, norm1_g, od_w_in, j, 512)
            ya = _attention(u, _att_bias_table(od_rpb[j]))
            yb = _pool(u, od_pool_w, od_pool_scale, j)
            xc, hp, ri = _proj(ya, yb, od_w_out_bf, j, ev_ln_g, ev_ln_b, xc, mods, i, norm2_g,
                               rw, rb, even=False)
        else:
            u = _norm_mm(xc, mods, i, norm1_g, ev_w_in, j, 512)
            ya, yb = _conv(u, ev_conv_a, ev_dw_w, ev_dw_b, j)
            xc, hp, ri = _proj(ya, yb, ev_w_out_bf, j, ev_ln_g, ev_ln_b, xc, mods, i, norm2_g,
                               rw, rb, even=True)
        pos, base = _positions(ri.reshape(NTOK, ROUTE_LANES))
        pos_flat = pos.reshape(2 * NTOK)
        items = _items(base[0, :N_EXPERTS + 1 + 7])
        xs = _scatter(pos_flat, hp.reshape(NTOK, HALF))
        ys = _gmm(items, xs, moe_w1, moe_w3, moe_w2, i)
        xc = _combine(pos_flat, ys, xc, ri, mods, i, final_g2, last)
    return xc
```

```python
import functools

import numpy as np
import jax
import jax.numpy as jnp
from jax import lax
from jax.experimental import pallas as pl
from jax.experimental.pallas import tpu as pltpu

F32 = jnp.float32
BF16 = jnp.bfloat16
I32 = jnp.int32
HIGHEST = lax.Precision.HIGHEST

D = 2048
B = 4
SEQ = 2048
CTX = 256
T = CTX + SEQ
NTOK = B * T
DEPTH = 4
GRID_W = 64
GRID_ROWS = SEQ // GRID_W
HALF = D // 2
SC_W = 1024
CF_KERNEL = 31
SC_WIDTH = 3
NA_HEADS = 8
NA_DH = 128
NA_ROWS = 8
NA_COLS = 16
N_POOL = 4
POOL_GC = 256
N_EXPERTS = 32
N_GROUPS = 4
EPG = 8
EXPERT_FF = 256
RMS_EPS = 1e-6
LN_EPS = 1e-5
NEG_INF = -1e30

ROW_TILE = 256
NSLOT = 2 * NTOK
N_SORT_TILES = NSLOT // ROW_TILE
N_ITEMS = 128
ITEM_NEW, ITEM_SLOT, ITEM_NEXT = 4 * N_ITEMS, 5 * N_ITEMS, 6 * N_ITEMS
NM_TILE = T // 2
NM_CHUNK = 16
NM_UNROLL = 8
ATT_QROWS = 4
ATT_Q = ATT_QROWS * GRID_W
ATT_KROWS = 12
ATT_K = ATT_KROWS * GRID_W
CONV_CW = 256
CONV_PAD = 16
CONV_ROWS = 32
VMEM_LIMIT = 56 * 1024 * 1024


def _sigmoid(v):
    return 1.0 / (1.0 + jnp.exp(-v))


ADA_TN = 1024


def _ada_kernel(c_ref, w_ref, b_ref, o_ref):
    cv = c_ref[...]
    s = cv * _sigmoid(cv)
    o_ref[0] = jnp.dot(s, w_ref[0], preferred_element_type=F32, precision=HIGHEST) + b_ref[0]


def _ada(c8, ada_w, ada_b):
    n = ada_w.shape[-1]
    return pl.pallas_call(
        _ada_kernel,
        grid=(DEPTH, n // ADA_TN),
        in_specs=[
            pl.BlockSpec((8, D), lambda l, j: (0, 0)),
            pl.BlockSpec((1, D, ADA_TN), lambda l, j: (l, 0, j)),
            pl.BlockSpec((1, 1, ADA_TN), lambda l, j: (l, 0, j)),
        ],
        out_specs=pl.BlockSpec((1, 8, ADA_TN), lambda l, j: (l, 0, j)),
        out_shape=jax.ShapeDtypeStruct((DEPTH, 8, n), F32),
        compiler_params=pltpu.CompilerParams(
            dimension_semantics=("arbitrary", "arbitrary"), vmem_limit_bytes=VMEM_LIMIT),
        name="ada",
    )(c8, ada_w, ada_b.reshape(DEPTH, 1, n))


def _modulated_rmsnorm_rows(x_ref, h_ref, gain_ref, shift_ref, first, count):
    def group(gi, carry):
        for k in range(NM_UNROLL):
            rows = pl.ds(pl.multiple_of((gi * NM_UNROLL + k) * NM_CHUNK, NM_CHUNK), NM_CHUNK)
            xv = x_ref[rows, :]
            inv = lax.rsqrt(jnp.mean(xv * xv, axis=-1, keepdims=True) + RMS_EPS)
            h_ref[rows, :] = (xv * inv * gain_ref[...] + shift_ref[...]).astype(BF16)
        return carry

    lax.fori_loop(first, first + count, group, 0)


def _norm_mm_kernel(x_ref, modb_ref, modc_ref, g_ref, w_ref, o_ref, h_ref, gain_ref, shift_ref):
    m = pl.program_id(1)
    n = pl.program_id(2)

    @pl.when(n == 0)
    def _():
        nchunk = NM_TILE // (NM_CHUNK * NM_UNROLL)
        n_ctx = jnp.where(m == 0, CTX // (NM_CHUNK * NM_UNROLL), 0)
        gain_ref[...] = g_ref[...] * (1.0 + modc_ref[1:2, :])
        shift_ref[...] = modc_ref[0:1, :]
        _modulated_rmsnorm_rows(x_ref, h_ref, gain_ref, shift_ref, 0, n_ctx)
        gain_ref[...] = g_ref[...] * (1.0 + modb_ref[1:2, :])
        shift_ref[...] = modb_ref[0:1, :]
        _modulated_rmsnorm_rows(x_ref, h_ref, gain_ref, shift_ref, n_ctx, nchunk - n_ctx)

    o_ref[...] = jnp.dot(h_ref[...], w_ref[...].astype(BF16),
                         preferred_element_type=F32).astype(BF16)


def _norm_mm(xc, mods, layer, g, w_all, widx, tn):
    n_out = w_all.shape[-1]
    return pl.pallas_call(
        _norm_mm_kernel,
        grid=(B, T // NM_TILE, n_out // tn),
        in_specs=[
            pl.BlockSpec((None, NM_TILE, D), lambda b, m, n: (b, m, 0)),
            pl.BlockSpec((None, None, 6, D), lambda b, m, n: (layer, b, 0, 0)),
            pl.BlockSpec((None, None, 6, D), lambda b, m, n: (layer, B, 0, 0)),
            pl.BlockSpec((None, 1, D), lambda b, m, n: (layer, 0, 0)),
            pl.BlockSpec((None, D, tn), lambda b, m, n: (widx, 0, n)),
        ],
        out_specs=pl.BlockSpec((None, NM_TILE, tn), lambda b, m, n: (b, m, n)),
        out_shape=jax.ShapeDtypeStruct((B, T, n_out), BF16),
        scratch_shapes=[pltpu.VMEM((NM_TILE, D), BF16),
                        pltpu.VMEM((1, D), F32), pltpu.VMEM((1, D), F32)],
        compiler_params=pltpu.CompilerParams(
            dimension_semantics=("arbitrary", "arbitrary", "arbitrary"),
            vmem_limit_bytes=VMEM_LIMIT),
        name="norm_mm",
    )(xc, mods, mods, g, w_all)


def _depthwise_taps(win, taps, width):
    nwin = CONV_ROWS + 2 * CONV_PAD
    acc = jnp.zeros((CONV_ROWS, win.shape[1]), F32)
    for res in range(8):
        offsets = [o for o in range(res, nwin - CONV_ROWS + 1, 8)
                   if 0 <= o - CONV_PAD + width // 2 < width]
        if not offsets:
            continue
        rolled = win if res == 0 else pltpu.roll(win, nwin - res, 0)
        for o in offsets:
            k = o - CONV_PAD + width // 2
            acc = acc + rolled[o - res:o - res + CONV_ROWS, :] * taps[k:k + 1, :]
    return acc


def _conv_kernel(bg_ref, cg_ref, xa_ref, ga_ref, gb_ref, ca_ref, dw_ref, dwb_ref,
                 ya_ref, zc_ref, vs_ref, zs_ref):
    ca = ca_ref[...]
    dw = dw_ref[...]
    dwb = dwb_ref[...]
    zeros_pad = jnp.zeros((CONV_PAD, CONV_CW), F32)

    def run_sequence(seq0, length):
        vs_ref[pl.ds(0, CONV_PAD), :] = zeros_pad
        zs_ref[pl.ds(0, CONV_PAD), :] = zeros_pad
        vs_ref[pl.ds(CONV_PAD + length, CONV_PAD), :] = zeros_pad
        zs_ref[pl.ds(CONV_PAD + length, CONV_PAD), :] = zeros_pad

        def fill(i, carry):
            r = pl.multiple_of(i * ROW_TILE, ROW_TILE)
            src = pl.ds(seq0 + r, ROW_TILE)
            dst = pl.ds(CONV_PAD + r, ROW_TILE)
            vs_ref[dst, :] = cg_ref[src, :].astype(F32) * xa_ref[src, :].astype(F32)
            zs_ref[dst, :] = ga_ref[src, :].astype(F32) * _sigmoid(gb_ref[src, :].astype(F32))
            return carry

        lax.fori_loop(0, length // ROW_TILE, fill, 0)

        def conv(i, carry):
            r = pl.multiple_of(i * CONV_ROWS, CONV_ROWS)
            win = pl.ds(r, CONV_ROWS + 2 * CONV_PAD)
            out = pl.ds(seq0 + r, CONV_ROWS)
            acc3 = _depthwise_taps(vs_ref[win, :], ca, SC_WIDTH)
            ya_ref[out, :] = (bg_ref[out, :].astype(F32) * acc3).astype(BF16)
            acc = _depthwise_taps(zs_ref[win, :], dw, CF_KERNEL)
            zc_ref[out, :] = (acc + dwb).astype(BF16)
            return carry

        lax.fori_loop(0, length // CONV_ROWS, conv, 0)

    run_sequence(0, CTX)
    run_sequence(CTX, SEQ)


def _conv(u, conv_a, dw_w, dw_b, j):
    nblk = SC_W // CONV_CW
    part = lambda k: pl.BlockSpec((None, T, CONV_CW), lambda b, cb: (b, 0, k * nblk + cb))
    out_spec = pl.BlockSpec((None, T, CONV_CW), lambda b, cb: (b, 0, cb))
    return pl.pallas_call(
        _conv_kernel,
        grid=(B, nblk),
        in_specs=[part(0), part(1), part(2), part(3), part(4),
                  pl.BlockSpec((None, SC_WIDTH, CONV_CW), lambda b, cb: (j, 0, cb)),
                  pl.BlockSpec((None, CF_KERNEL, CONV_CW), lambda b, cb: (j, 0, cb)),
                  pl.BlockSpec((None, 1, CONV_CW), lambda b, cb: (j, 0, cb))],
        out_specs=[out_spec, out_spec],
        out_shape=[jax.ShapeDtypeStruct((B, T, HALF), BF16)] * 2,
        scratch_shapes=[pltpu.VMEM((SEQ + 2 * CONV_PAD, CONV_CW), F32)] * 2,
        compiler_params=pltpu.CompilerParams(
            dimension_semantics=("arbitrary", "arbitrary"), vmem_limit_bytes=VMEM_LIMIT),
        name="conv_mixers",
    )(u, u, u, u, u, conv_a, dw_w, dw_b.reshape(dw_b.shape[0], 1, dw_b.shape[1]))


N_RPB_DR = 2 * NA_ROWS - 1
N_RPB_DC = 2 * NA_COLS - 1
ATT_MASKED = N_RPB_DR


def _att_block_geometry(blk):
    r0 = blk * ATT_QROWS
    return min(max(r0 - NA_ROWS // 2, 0), GRID_ROWS - ATT_KROWS)


def _softmax_pv(parts):
    mx = None
    for s, _ in parts:
        pm = jnp.max(s, axis=-1, keepdims=True)
        mx = pm if mx is None else jnp.maximum(mx, pm)
    den = None
    num = None
    for s, v in parts:
        e = jnp.exp(s - mx)
        ps = jnp.sum(e, axis=-1, keepdims=True)
        pv = jnp.dot(e.astype(BF16), v, preferred_element_type=F32)
        den = ps if den is None else den + ps
        num = pv if num is None else num + pv
    return num / den


def _att_kernel(rpb_ref, q_ref, k_ref, v_ref, o_ref, tile_ref, *, rpb_base):
    @pl.when(pl.program_id(1) == 0)
    def _():
        base = (rpb_base + pl.program_id(0)) * (N_RPB_DR * N_RPB_DC)
        ci = lax.broadcasted_iota(I32, (GRID_W, 2 * GRID_W), 0)
        wi = lax.broadcasted_iota(I32, (GRID_W, 2 * GRID_W), 1) & (GRID_W - 1)
        cs = jnp.clip(ci - NA_COLS // 2, 0, GRID_W - NA_COLS)
        dsel = jnp.where((wi >= cs) & (wi < cs + NA_COLS), wi - ci + (NA_COLS - 1), -1)
        masked = jnp.full((GRID_W, 2 * GRID_W), NEG_INF, F32)
        for dr in range(N_RPB_DR):
            t = masked
            for dc in range(N_RPB_DC):
                t = jnp.where(dsel == dc, rpb_ref[base + dr * N_RPB_DC + dc], t)
            tile_ref[dr] = t
        tile_ref[ATT_MASKED] = masked

    def block_bias(blk, kr0):
        low_half = lax.broadcasted_iota(I32, (GRID_W, 2 * GRID_W), 1) < GRID_W
        bands = []
        for qr in range(ATT_QROWS):
            r = blk * ATT_QROWS + qr
            rs = min(max(r - NA_ROWS // 2, 0), GRID_ROWS - NA_ROWS)
            tiles = []
            for jp in range(ATT_KROWS // 2):
                idx = [kr - r + (NA_ROWS - 1) if rs <= kr < rs + NA_ROWS else ATT_MASKED
                       for kr in (kr0 + 2 * jp, kr0 + 2 * jp + 1)]
                if idx[0] == idx[1]:
                    tiles.append(tile_ref[idx[0]])
                else:
                    tiles.append(jnp.where(low_half, tile_ref[idx[0]], tile_ref[idx[1]]))
            bands.append(jnp.concatenate(tiles, axis=1))
        return jnp.concatenate(bands, axis=0)

    scale = NA_DH ** -0.5
    nt = (((1,), (1,)), ((), ()))
    kc = k_ref[pl.ds(0, CTX), :]
    vc = v_ref[pl.ds(0, CTX), :]

    qc = q_ref[pl.ds(0, CTX), :]
    s_cc = lax.dot_general(qc, kc, nt, preferred_element_type=F32) * scale
    o_ref[pl.ds(0, CTX), :] = _softmax_pv([(s_cc, vc)]).astype(BF16)

    for blk in range(GRID_ROWS // ATT_QROWS):
        kr0 = _att_block_geometry(blk)
        qrows = pl.ds(CTX + blk * ATT_Q, ATT_Q)
        krows = pl.ds(CTX + kr0 * GRID_W, ATT_K)
        qb = q_ref[qrows, :]
        s_w = lax.dot_general(qb, k_ref[krows, :], nt, preferred_element_type=F32) * scale
        s_w = s_w + block_bias(blk, kr0)
        s_c = lax.dot_general(qb, kc, nt, preferred_element_type=F32) * scale
        o_ref[qrows, :] = _softmax_pv([(s_w, v_ref[krows, :]), (s_c, vc)]).astype(BF16)


def _attention(u, rpb_flat, j):
    col = lambda off: pl.BlockSpec((None, T, NA_DH), lambda h, b: (b, 0, off + h))
    return pl.pallas_call(
        functools.partial(_att_kernel, rpb_base=j * NA_HEADS),
        grid=(NA_HEADS, B),
        in_specs=[pl.BlockSpec(memory_space=pltpu.SMEM),
                  col(0), col(NA_HEADS), col(2 * NA_HEADS)],
        out_specs=pl.BlockSpec((None, T, NA_DH), lambda h, b: (b, 0, h)),
        out_shape=jax.ShapeDtypeStruct((B, T, HALF), BF16),
        scratch_shapes=[pltpu.VMEM((N_RPB_DR + 1, GRID_W, 2 * GRID_W), F32)],
        compiler_params=pltpu.CompilerParams(
            dimension_semantics=("arbitrary", "arbitrary"), vmem_limit_bytes=VMEM_LIMIT),
        name="neighbourhood_attention",
    )(rpb_flat, u, u, u)


POOL_KWIN = 512


def _pool_kernel(p_ref, pw_ref, ps_ref, o_ref):
    half = lax.shift_left(jnp.int32(1), pl.program_id(1))
    pw = pw_ref[...].astype(BF16)
    sc = ps_ref[...]

    def tile(row0, seq0, length, ks, kwin):
        t = (row0 - seq0) + lax.broadcasted_iota(I32, (ROW_TILE, 1), 0)
        s = ks + lax.broadcasted_iota(I32, (1, kwin), 1)
        lo = t - half
        hi = t + half - 1
        band = jnp.where((s >= lo) & (s <= hi), 1.0, 0.0).astype(BF16)
        ssum = jnp.dot(band, p_ref[pl.ds(seq0 + ks, kwin), :], preferred_element_type=F32)
        cnt = (jnp.minimum(hi, length - 1) - jnp.maximum(lo, 0) + 1).astype(F32)
        mix = ssum / cnt - p_ref[pl.ds(row0, ROW_TILE), :].astype(F32)
        y = jnp.dot(mix.astype(BF16), pw, preferred_element_type=F32) * sc
        o_ref[pl.ds(row0, ROW_TILE), :] = y.astype(BF16)

    tile(0, 0, CTX, 0, CTX)
    for jt in range(SEQ // ROW_TILE):
        ks = min(max(ROW_TILE * jt - (POOL_KWIN - ROW_TILE) // 2, 0), SEQ - POOL_KWIN)
        tile(CTX + ROW_TILE * jt, CTX, SEQ, ks, POOL_KWIN)


def _pool(u, pool_w, pool_scale, j):
    p_block0 = 3 * HALF // POOL_GC
    return pl.pallas_call(
        _pool_kernel,
        grid=(B, N_POOL),
        in_specs=[pl.BlockSpec((None, T, POOL_GC), lambda b, g: (b, 0, p_block0 + g)),
                  pl.BlockSpec((None, None, POOL_GC, POOL_GC), lambda b, g: (j, g, 0, 0)),
                  pl.BlockSpec((None, None, 1, POOL_GC), lambda b, g: (j, g, 0, 0))],
        out_specs=pl.BlockSpec((None, T, POOL_GC), lambda b, g: (b, 0, g)),
        out_shape=jax.ShapeDtypeStruct((B, T, HALF), BF16),
        compiler_params=pltpu.CompilerParams(
            dimension_semantics=("arbitrary", "arbitrary"), vmem_limit_bytes=VMEM_LIMIT),
        name="pool_mixer",
    )(u, pool_w, pool_scale.reshape(pool_scale.shape[0], N_POOL, 1, POOL_GC))


ROUTE_LANES = 128


def _route(logits):
    lane = lax.broadcasted_iota(I32, logits.shape, 1)
    lanef = lane.astype(F32)
    is_g = lane < N_GROUPS
    gl = jnp.where(is_g, logits, NEG_INF)
    ge = jnp.where(is_g, jnp.exp(gl - jnp.max(gl, axis=-1, keepdims=True)), 0.0)
    gp = ge / jnp.sum(ge, axis=-1, keepdims=True)
    g_p = jnp.max(gp, axis=-1, keepdims=True)
    g_idx = jnp.min(jnp.where(is_g & (gp == g_p), lanef, float(ROUTE_LANES)), axis=-1, keepdims=True)
    first = float(N_GROUPS) + float(EPG) * g_idx
    sel = (lanef >= first) & (lanef < first + float(EPG))
    el = jnp.where(sel, logits, NEG_INF)
    ee = jnp.where(sel, jnp.exp(el - jnp.max(el, axis=-1, keepdims=True)), 0.0)
    ep = jnp.where(sel, ee / jnp.sum(ee, axis=-1, keepdims=True), -1.0)
    p1 = jnp.max(ep, axis=-1, keepdims=True)
    i1 = jnp.min(jnp.where(ep == p1, lanef, float(ROUTE_LANES)), axis=-1, keepdims=True)
    ep2 = jnp.where(lanef == i1, -1.0, ep)
    p2 = jnp.max(ep2, axis=-1, keepdims=True)
    i2 = jnp.min(jnp.where(ep2 == p2, lanef, float(ROUTE_LANES)), axis=-1, keepdims=True)
    psum = p1 + p2
    gate1 = g_p * p1 / psum
    gate2 = g_p * p2 / psum
    return jnp.where(lane == 0, i1 - float(N_GROUPS),
                     jnp.where(lane == 1, i2 - float(N_GROUPS),
                               jnp.where(lane == 2, gate1,
                                         jnp.where(lane == 3, gate2, 0.0))))


def _proj_kernel(ya_ref, yb_ref, w_ref, lng_ref, lnb_ref, x_ref, modb_ref, modc_ref, g2_ref,
                 rwh_ref, rwl_ref, rb_ref, xo_ref, hp_ref, ri_ref,
                 yb_s, out_s, hh_s, hl_s, vec_s, *, even):
    nchunk = ROW_TILE // NM_CHUNK

    def rows_of(c):
        return pl.ds(c * NM_CHUNK, NM_CHUNK)

    if even:
        def layernorm_swish(c, carry):
            rows = rows_of(c)
            zc = yb_ref[rows, :].astype(F32)
            dz = zc - jnp.mean(zc, axis=-1, keepdims=True)
            var = jnp.mean(dz * dz, axis=-1, keepdims=True)
            z = dz * lax.rsqrt(var + LN_EPS) * lng_ref[...] + lnb_ref[...]
            yb_s[rows, :] = (z * _sigmoid(z)).astype(BF16)
            return carry

        for c in range(nchunk):
            layernorm_swish(c, 0)
        yb = yb_s[...]
    else:
        yb = yb_ref[...]
    out_s[...] = (jnp.dot(ya_ref[...], w_ref[pl.ds(0, HALF), :], preferred_element_type=F32)
                  + jnp.dot(yb, w_ref[pl.ds(HALF, HALF), :], preferred_element_type=F32))

    mod = jnp.where(pl.program_id(1) == 0, modc_ref[...], modb_ref[...])
    vec_s[0:1, :] = mod[2:3, :]
    vec_s[1:2, :] = g2_ref[...] * (1.0 + mod[4:5, :])
    vec_s[2:3, :] = mod[3:4, :]

    def residual_norm(c, carry):
        rows = rows_of(c)
        xn = x_ref[rows, :] + vec_s[0:1, :] * out_s[rows, :]
        xo_ref[rows, :] = xn
        inv = lax.rsqrt(jnp.mean(xn * xn, axis=-1, keepdims=True) + RMS_EPS)
        h2 = xn * inv * vec_s[1:2, :] + vec_s[2:3, :]
        hp_ref[rows, :] = h2
        hb = h2.astype(BF16)
        hh_s[rows, :] = hb
        hl_s[rows, :] = (h2 - hb.astype(F32)).astype(BF16)
        return carry

    for c in range(nchunk):
        residual_norm(c, 0)

    hh = hh_s[...]
    logits = (jnp.dot(hh, rwh_ref[...], preferred_element_type=F32)
              + jnp.dot(hl_s[...], rwh_ref[...], preferred_element_type=F32)
              + jnp.dot(hh, rwl_ref[...], preferred_element_type=F32)) + rb_ref[...]
    ri_ref[...] = _route(logits)


def _proj(ya, yb, w_bf, widx, ln_g, ln_b, xc, mods, layer, g2, rw_hi, rw_lo, rb, even):
    row = lambda width: pl.BlockSpec((None, ROW_TILE, width), lambda b, m: (b, m, 0))
    return pl.pallas_call(
        functools.partial(_proj_kernel, even=even),
        grid=(B, T // ROW_TILE),
        in_specs=[row(HALF), row(HALF),
                  pl.BlockSpec((None, D, D), lambda b, m: (widx, 0, 0)),
                  pl.BlockSpec((None, 1, HALF), lambda b, m: (widx, 0, 0)),
                  pl.BlockSpec((None, 1, HALF), lambda b, m: (widx, 0, 0)),
                  row(D),
                  pl.BlockSpec((None, None, 6, D), lambda b, m: (layer, b, 0, 0)),
                  pl.BlockSpec((None, None, 6, D), lambda b, m: (layer, B, 0, 0)),
                  pl.BlockSpec((None, 1, D), lambda b, m: (layer, 0, 0)),
                  pl.BlockSpec((None, D, ROUTE_LANES), lambda b, m: (layer, 0, 0)),
                  pl.BlockSpec((None, D, ROUTE_LANES), lambda b, m: (layer, 0, 0)),
                  pl.BlockSpec((None, 1, ROUTE_LANES), lambda b, m: (layer, 0, 0))],
        out_specs=[row(D), row(D), row(ROUTE_LANES)],
        out_shape=[jax.ShapeDtypeStruct((B, T, D), F32),
                   jax.ShapeDtypeStruct((B, T, D), F32),
                   jax.ShapeDtypeStruct((B, T, ROUTE_LANES), F32)],
        scratch_shapes=[pltpu.VMEM((ROW_TILE, HALF), BF16),
                        pltpu.VMEM((ROW_TILE, D), F32),
                        pltpu.VMEM((ROW_TILE, D), BF16),
                        pltpu.VMEM((ROW_TILE, D), BF16),
                        pltpu.VMEM((8, D), F32)],
        compiler_params=pltpu.CompilerParams(
            dimension_semantics=("arbitrary", "arbitrary"), vmem_limit_bytes=VMEM_LIMIT),
        name="out_proj_router",
    )(ya, yb, w_bf, ln_g, ln_b, xc, mods, mods, g2, rw_hi, rw_lo, rb)


def _positions_kernel(ri_ref, pos_ref, base_ref):
    ntiles = NTOK // ROW_TILE
    lanef = lax.broadcasted_iota(I32, (ROW_TILE, ROUTE_LANES), 1).astype(F32)

    def count(i, acc):
        r = ri_ref[pl.ds(pl.multiple_of(i * ROW_TILE, ROW_TILE), ROW_TILE), :]
        hit = (lanef == r[:, 0:1]) | (lanef == r[:, 1:2])
        return acc + jnp.sum(jnp.where(hit, 1.0, 0.0), axis=0, keepdims=True)

    cnt = lax.fori_loop(0, ntiles, count, jnp.zeros((1, ROUTE_LANES), F32))
    ri = lax.broadcasted_iota(I32, (ROUTE_LANES, ROUTE_LANES), 0)
    ci = lax.broadcasted_iota(I32, (ROUTE_LANES, ROUTE_LANES), 1)
    before = jnp.where(ri < ci, 1.0, 0.0)
    cnt8 = jnp.broadcast_to(cnt, (8, ROUTE_LANES))
    base8 = jnp.dot(cnt8, before, preferred_element_type=F32, precision=HIGHEST)
    base_ref[...] = base8.astype(I32)
    base_col = jnp.broadcast_to(base8[0:1, :], (ROUTE_LANES, ROUTE_LANES)).T[:, 0:1]

    ti = lax.broadcasted_iota(I32, (ROW_TILE, ROW_TILE), 0)
    tj = lax.broadcasted_iota(I32, (ROW_TILE, ROW_TILE), 1)
    earlier = jnp.where(ti < tj, 1.0, 0.0).astype(BF16)
    subf = lax.broadcasted_iota(I32, (ROUTE_LANES, ROW_TILE), 0).astype(F32)

    def place(i, carry):
        start = pl.multiple_of(i * ROW_TILE, ROW_TILE)
        rt = ri_ref[pl.ds(start, ROW_TILE), :].T
        e0 = rt[0:1, :]
        e1 = rt[1:2, :]
        hit0 = subf == e0
        hit1 = subf == e1
        hits = jnp.where(hit0 | hit1, 1.0, 0.0)
        rank = jnp.dot(hits.astype(BF16), earlier, preferred_element_type=F32) + carry + base_col
        pos_ref[0:1, pl.ds(start, ROW_TILE)] = jnp.sum(
            jnp.where(hit0, rank, 0.0), axis=0, keepdims=True).astype(I32)
        pos_ref[1:2, pl.ds(start, ROW_TILE)] = jnp.sum(
            jnp.where(hit1, rank, 0.0), axis=0, keepdims=True).astype(I32)
        return carry + jnp.sum(hits, axis=1, keepdims=True)

    lax.fori_loop(0, ntiles, place, jnp.zeros((ROUTE_LANES, 1), F32))


def _positions(ri_flat):
    return pl.pallas_call(
        _positions_kernel,
        out_shape=[jax.ShapeDtypeStruct((2, NTOK), I32),
                   jax.ShapeDtypeStruct((8, ROUTE_LANES), I32)],
        compiler_params=pltpu.CompilerParams(vmem_limit_bytes=VMEM_LIMIT),
        name="moe_positions",
    )(ri_flat)


def _items_kernel(base_ref, it_ref):
    def init(i, carry):
        it_ref[ITEM_NEW + i] = 0
        it_ref[ITEM_SLOT + i] = 0
        it_ref[ITEM_NEXT + i] = -1
        return carry

    lax.fori_loop(0, N_ITEMS, init, 0)

    def per_expert(e, carry):
        n, run, run_start = carry
        s = base_ref[e]
        t = base_ref[e + 1]
        first = lax.shift_right_logical(s, 8)
        last = lax.shift_right_logical(jnp.maximum(t - 1, 0), 8)
        used = t > s
        ntile = jnp.where(used, last - first + 1, 0)

        def per_tile(k, n2):
            tile = first + k
            it_ref[n2] = tile
            it_ref[N_ITEMS + n2] = e
            it_ref[2 * N_ITEMS + n2] = jnp.maximum(s - tile * ROW_TILE, 0)
            it_ref[3 * N_ITEMS + n2] = jnp.minimum(t - tile * ROW_TILE, ROW_TILE)
            return n2 + 1

        n_end = lax.fori_loop(0, ntile, per_tile, n)
        it_ref[ITEM_NEW + n] = jnp.where(used, 1, it_ref[ITEM_NEW + n])
        it_ref[ITEM_SLOT + n] = jnp.where(used, run & 1, it_ref[ITEM_SLOT + n])
        link = used & (run > 0)
        it_ref[ITEM_NEXT + run_start] = jnp.where(link, e, it_ref[ITEM_NEXT + run_start])
        return n_end, run + jnp.where(used, 1, 0), jnp.where(used, n, run_start)

    n_used, _, _ = lax.fori_loop(0, N_EXPERTS, per_expert,
                                 (jnp.int32(0), jnp.int32(0), jnp.int32(0)))
    last_e = it_ref[N_ITEMS + n_used - 1]

    def pad(i, carry):
        it_ref[i] = N_SORT_TILES - 1
        it_ref[N_ITEMS + i] = last_e
        it_ref[2 * N_ITEMS + i] = 0
        it_ref[3 * N_ITEMS + i] = 0
        return carry

    lax.fori_loop(n_used, N_ITEMS, pad, 0)


def _items(base_flat):
    return pl.pallas_call(
        _items_kernel,
        in_specs=[pl.BlockSpec(memory_space=pltpu.SMEM)],
        out_specs=pl.BlockSpec(memory_space=pltpu.SMEM),
        out_shape=jax.ShapeDtypeStruct((7 * N_ITEMS,), I32),
        name="moe_items",
    )(base_flat)


SCATTER_UNROLL = 4


def _scatter_kernel(pos_ref, h_hbm, xs_hbm, buf, load_sems, row_sems):
    i = pl.program_id(0)
    n = pl.num_programs(0)
    par = i % 2

    def load(tile, p):
        return pltpu.make_async_copy(h_hbm.at[pl.ds(tile * ROW_TILE, ROW_TILE)], buf.at[p],
                                     load_sems.at[p])

    def wait_rows(p):
        for _ in range(2):
            pltpu.make_async_copy(buf.at[p], xs_hbm.at[pl.ds(0, ROW_TILE)], row_sems.at[p]).wait()

    @pl.when(i == 0)
    def _():
        load(0, 0).start()

    load(i, par).wait()

    @pl.when(i >= 1)
    def _():
        wait_rows(1 - par)

    @pl.when(i + 1 < n)
    def _():
        load(i + 1, 1 - par).start()

    def rows(g, carry):
        for k in range(SCATTER_UNROLL):
            r = g * SCATTER_UNROLL + k
            t = i * ROW_TILE + r
            for s in range(2):
                pltpu.make_async_copy(buf.at[par, pl.ds(r, 1)],
                                      xs_hbm.at[pl.ds(pos_ref[s * NTOK + t], 1)],
                                      row_sems.at[par]).start()
        return carry

    lax.fori_loop(0, ROW_TILE // SCATTER_UNROLL, rows, 0)

    @pl.when(i == n - 1)
    def _():
        wait_rows(par)


def _scatter(pos_flat, hp_flat):
    return pl.pallas_call(
        _scatter_kernel,
        grid_spec=pltpu.PrefetchScalarGridSpec(
            num_scalar_prefetch=1,
            grid=(NTOK // ROW_TILE,),
            in_specs=[pl.BlockSpec(memory_space=pl.ANY)],
            out_specs=pl.BlockSpec(memory_space=pl.ANY),
            scratch_shapes=[pltpu.VMEM((2, ROW_TILE, D), F32),
                            pltpu.SemaphoreType.DMA((2,)),
                            pltpu.SemaphoreType.DMA((2,))],
        ),
        out_shape=jax.ShapeDtypeStruct((NSLOT, D), F32),
        compiler_params=pltpu.CompilerParams(
            dimension_semantics=("arbitrary",), has_side_effects=True),
        name="moe_scatter",
    )(pos_flat, hp_flat)


def _gmm_kernel(it_ref, xs_ref, w1_hbm, w3_hbm, w2_hbm, o_ref,
                w1_raw, w3_raw, w2_raw, w1_bf, w3_bf, w2_bf, sems, *, layer):
    i = pl.program_id(0)
    tile = it_ref[i]
    expert = it_ref[N_ITEMS + i]
    lo = it_ref[2 * N_ITEMS + i]
    hi = it_ref[3 * N_ITEMS + i]
    slot = it_ref[ITEM_SLOT + i]
    nxt = it_ref[ITEM_NEXT + i]
    prev_tile = it_ref[jnp.maximum(i - 1, 0)]

    def weight_copies(e, s):
        return (pltpu.make_async_copy(w1_hbm.at[layer, e], w1_raw.at[s], sems.at[s, 0]),
                pltpu.make_async_copy(w3_hbm.at[layer, e], w3_raw.at[s], sems.at[s, 1]),
                pltpu.make_async_copy(w2_hbm.at[layer, e], w2_raw.at[s], sems.at[s, 2]))

    @pl.when(it_ref[ITEM_NEW + i] == 1)
    def _():
        @pl.when(i == 0)
        def _():
            for cp in weight_copies(expert, slot):
                cp.start()

        for cp in weight_copies(expert, slot):
            cp.wait()

        @pl.when(nxt >= 0)
        def _():
            for cp in weight_copies(nxt, 1 - slot):
                cp.start()

        w1_bf[...] = w1_raw[slot].astype(BF16)
        w3_bf[...] = w3_raw[slot].astype(BF16)
        w2_bf[...] = w2_raw[slot].astype(BF16)

    @pl.when((i == 0) | (tile != prev_tile))
    def _():
        o_ref[...] = jnp.zeros_like(o_ref)

    @pl.when(hi > lo)
    def _():
        xb = xs_ref[...].astype(BF16)
        a = jnp.dot(xb, w1_bf[...], preferred_element_type=F32)
        b = jnp.dot(xb, w3_bf[...], preferred_element_type=F32)
        rows = lax.broadcasted_iota(I32, (ROW_TILE, 1), 0)
        hid = jnp.where((rows >= lo) & (rows < hi), a * _sigmoid(a) * b, 0.0)
        o_ref[...] += jnp.dot(hid.astype(BF16), w2_bf[...], preferred_element_type=F32)


def _gmm(items, xs, w1, w3, w2, layer):
    return pl.pallas_call(
        functools.partial(_gmm_kernel, layer=layer),
        grid_spec=pltpu.PrefetchScalarGridSpec(
            num_scalar_prefetch=1,
            grid=(N_ITEMS,),
            in_specs=[
                pl.BlockSpec((ROW_TILE, D), lambda i, it: (it[i], 0)),
                pl.BlockSpec(memory_space=pl.ANY),
                pl.BlockSpec(memory_space=pl.ANY),
                pl.BlockSpec(memory_space=pl.ANY),
            ],
            out_specs=pl.BlockSpec((ROW_TILE, D), lambda i, it: (it[i], 0)),
            scratch_shapes=[pltpu.VMEM((2, D, EXPERT_FF), F32),
                            pltpu.VMEM((2, D, EXPERT_FF), F32),
                            pltpu.VMEM((2, EXPERT_FF, D), F32),
                            pltpu.VMEM((D, EXPERT_FF), BF16),
                            pltpu.VMEM((D, EXPERT_FF), BF16),
                            pltpu.VMEM((EXPERT_FF, D), BF16),
                            pltpu.SemaphoreType.DMA((2, 3))],
        ),
        out_shape=jax.ShapeDtypeStruct((NSLOT, D), F32),
        compiler_params=pltpu.CompilerParams(
            dimension_semantics=("arbitrary",), vmem_limit_bytes=VMEM_LIMIT),
        name="moe_experts",
    )(items, xs, w1, w3, w2)


def _combine_kernel(pos_ref, ys_hbm, x_ref, ri_ref, modb_ref, modc_ref, fg_ref, o_ref, buf, gate_s, sems,
                    *, final, tiles_per_b, m_off):
    i = pl.program_id(0)
    ntiles = pl.num_programs(0)

    def issue(tile_i, par):
        t0 = (tile_i // tiles_per_b) * T + (tile_i % tiles_per_b + m_off) * ROW_TILE

        def rows(g, carry):
            for k in range(SCATTER_UNROLL):
                r = g * SCATTER_UNROLL + k
                for s in range(2):
                    pltpu.make_async_copy(ys_hbm.at[pl.ds(pos_ref[s * NTOK + t0 + r], 1)],
                                          buf.at[par, s, pl.ds(r, 1)], sems.at[par]).start()
            return carry

        lax.fori_loop(0, ROW_TILE // SCATTER_UNROLL, rows, 0)

    @pl.when(i == 0)
    def _():
        issue(0, 0)

    @pl.when(i + 1 < ntiles)
    def _():
        issue(i + 1, (i + 1) % 2)

    par = i % 2
    pltpu.make_async_copy(buf.at[par], buf.at[par], sems.at[par]).wait()
    m = i % tiles_per_b + m_off
    gate_s[...] = jnp.where(m == 0, modc_ref[5:6, :], modb_ref[5:6, :])
    for c in range(ROW_TILE // NM_CHUNK):
        rows = pl.ds(c * NM_CHUNK, NM_CHUNK)
        g0 = ri_ref[rows, 2:3]
        g1 = ri_ref[rows, 3:4]
        xn = x_ref[rows, :] + gate_s[...] * (g0 * buf[par, 0, rows, :] + g1 * buf[par, 1, rows, :])
        if final:
            inv = lax.rsqrt(jnp.mean(xn * xn, axis=-1, keepdims=True) + RMS_EPS)
            xn = xn * inv * fg_ref[...]
        o_ref[rows, :] = xn


def _combine(pos_flat, ys, xc, ri, mods, layer, final_g, final):
    if final:
        tiles_per_b, m_off, rows_out = SEQ // ROW_TILE, CTX // ROW_TILE, SEQ
    else:
        tiles_per_b, m_off, rows_out = T // ROW_TILE, 0, T
    tok = lambda width: pl.BlockSpec(
        (None, ROW_TILE, width), lambda i, pos: (i // tiles_per_b, i % tiles_per_b + m_off, 0))
    return pl.pallas_call(
        functools.partial(_combine_kernel, final=final, tiles_per_b=tiles_per_b, m_off=m_off),
        grid_spec=pltpu.PrefetchScalarGridSpec(
            num_scalar_prefetch=1,
            grid=(B * tiles_per_b,),
            in_specs=[
                pl.BlockSpec(memory_space=pl.ANY),
                tok(D),
                tok(ROUTE_LANES),
                pl.BlockSpec((None, None, 6, D), lambda i, pos: (layer, i // tiles_per_b, 0, 0)),
                pl.BlockSpec((None, None, 6, D), lambda i, pos: (layer, B, 0, 0)),
                pl.BlockSpec((1, D), lambda i, pos: (0, 0)),
            ],
            out_specs=pl.BlockSpec((None, ROW_TILE, D),
                                   lambda i, pos: (i // tiles_per_b, i % tiles_per_b, 0)),
            scratch_shapes=[pltpu.VMEM((2, 2, ROW_TILE, D), F32),
                            pltpu.VMEM((1, D), F32),
                            pltpu.SemaphoreType.DMA((2,))],
        ),
        out_shape=jax.ShapeDtypeStruct((B, rows_out, D), F32),
        compiler_params=pltpu.CompilerParams(
            dimension_semantics=("arbitrary",), vmem_limit_bytes=VMEM_LIMIT),
        name="moe_combine",
    )(pos_flat, ys, xc, ri, mods, mods, final_g)


def kernel(x, c, ctx, c_ctx, ada_w, ada_b, norm1_g, norm2_g, ev_w_in, ev_conv_a, ev_dw_w, ev_dw_b,
           ev_ln_g, ev_ln_b, ev_w_out, od_w_in, od_rpb, od_pool_w, od_pool_scale, od_w_out,
           moe_rg_w, moe_rg_b, moe_re_w, moe_re_b, moe_w1, moe_w3, moe_w2, final_g):
    xc = jnp.concatenate([ctx, x], axis=1)
    c8 = jnp.concatenate([c, c_ctx[None, :], jnp.zeros((8 - B - 1, D), F32)], axis=0)
    mods = _ada(c8, ada_w, ada_b).reshape(DEPTH, 8, 6, D)

    ev_w_out_bf = ev_w_out.astype(BF16)
    od_w_out_bf = od_w_out.astype(BF16)
    pad = ROUTE_LANES - N_GROUPS - N_EXPERTS
    rw = jnp.concatenate([moe_rg_w, moe_re_w, jnp.zeros((DEPTH, D, pad), F32)], axis=-1)
    rw_hi = rw.astype(BF16)
    rw_lo = (rw - rw_hi.astype(F32)).astype(BF16)
    rb = jnp.concatenate([moe_rg_b, moe_re_b, jnp.zeros((DEPTH, pad), F32)],
                         axis=-1).reshape(DEPTH, 1, ROUTE_LANES)
    final_g2 = final_g.reshape(1, D)
    norm1_g = norm1_g.reshape(DEPTH, 1, D)
    norm2_g = norm2_g.reshape(DEPTH, 1, D)
    ev_ln_g = ev_ln_g.reshape(-1, 1, HALF)
    ev_ln_b = ev_ln_b.reshape(-1, 1, HALF)

    for i in range(DEPTH):
        j = i // 2
        last = i == DEPTH - 1
        if i % 2 == 1:
            u = _norm_mm(xc, mods, i, norm1_g, od_w_in, j, 512)
            ya = _attention(u, od_rpb.reshape(-1), j)
            yb = _pool(u, od_pool_w, od_pool_scale, j)
            xc, hp, ri = _proj(ya, yb, od_w_out_bf, j, ev_ln_g, ev_ln_b, xc, mods, i, norm2_g,
                               rw_hi, rw_lo, rb, even=False)
        else:
            u = _norm_mm(xc, mods, i, norm1_g, ev_w_in, j, 512)
            ya, yb = _conv(u, ev_conv_a, ev_dw_w, ev_dw_b, j)
            xc, hp, ri = _proj(ya, yb, ev_w_out_bf, j, ev_ln_g, ev_ln_b, xc, mods, i, norm2_g,
                               rw_hi, rw_lo, rb, even=True)
        pos, base = _positions(ri.reshape(NTOK, ROUTE_LANES))
        pos_flat = pos.reshape(2 * NTOK)
        items = _items(base[0, :N_EXPERTS + 1 + 7])
        xs = _scatter(pos_flat, hp.reshape(NTOK, D))
        ys = _gmm(items, xs, moe_w1, moe_w3, moe_w2, i)
        xc = _combine(pos_flat, ys, xc, ri, mods, i, final_g2, last)
    return xc
```

```python
import functools

import numpy as np
import jax
import jax.numpy as jnp
from jax import lax
from jax.experimental import pallas as pl
from jax.experimental.pallas import tpu as pltpu

F32 = jnp.float32
BF16 = jnp.bfloat16
I32 = jnp.int32
HIGHEST = lax.Precision.HIGHEST

D = 2048
B = 4
SEQ = 2048
CTX = 256
T = CTX + SEQ
NTOK = B * T
DEPTH = 4
GRID_W = 64
GRID_ROWS = SEQ // GRID_W
HALF = D // 2
SC_W = 1024
CF_KERNEL = 31
SC_WIDTH = 3
NA_HEADS = 8
NA_DH = 128
NA_ROWS = 8
NA_COLS = 16
N_POOL = 4
POOL_GC = 256
N_EXPERTS = 32
N_GROUPS = 4
EPG = 8
EXPERT_FF = 256
RMS_EPS = 1e-6
LN_EPS = 1e-5
NEG_INF = -1e30

ROW_TILE = 256
NSLOT = 2 * NTOK
N_SORT_TILES = NSLOT // ROW_TILE
N_ITEMS = 128
ITEM_NEW, ITEM_SLOT, ITEM_NEXT = 4 * N_ITEMS, 5 * N_ITEMS, 6 * N_ITEMS
NM_TN = 1024
NM_CHUNK = 16
ATT_QROWS = 4
ATT_Q = ATT_QROWS * GRID_W
ATT_KROWS = 12
ATT_K = ATT_KROWS * GRID_W
CONV_CW = 256
CONV_PAD = 16
CONV_ROWS = 32
VMEM_LIMIT = 56 * 1024 * 1024


def _sigmoid(v):
    return 1.0 / (1.0 + jnp.exp(-v))


ADA_TN = 1024


def _ada_kernel(c_ref, w_ref, b_ref, o_ref):
    cv = c_ref[...]
    s = cv * _sigmoid(cv)
    o_ref[0] = jnp.dot(s, w_ref[0], preferred_element_type=F32, precision=HIGHEST) + b_ref[0]


def _ada(c8, ada_w, ada_b):
    n = ada_w.shape[-1]
    return pl.pallas_call(
        _ada_kernel,
        grid=(DEPTH, n // ADA_TN),
        in_specs=[
            pl.BlockSpec((8, D), lambda l, j: (0, 0)),
            pl.BlockSpec((1, D, ADA_TN), lambda l, j: (l, 0, j)),
            pl.BlockSpec((1, 1, ADA_TN), lambda l, j: (l, 0, j)),
        ],
        out_specs=pl.BlockSpec((1, 8, ADA_TN), lambda l, j: (l, 0, j)),
        out_shape=jax.ShapeDtypeStruct((DEPTH, 8, n), F32),
        compiler_params=pltpu.CompilerParams(
            dimension_semantics=("arbitrary", "arbitrary"), vmem_limit_bytes=VMEM_LIMIT),
        name="ada",
    )(c8, ada_w, ada_b.reshape(DEPTH, 1, n))


def _norm_mm_kernel(x_ref, modb_ref, modc_ref, g_ref, w_ref, o_ref, h_ref, gain_ref, shift_ref):
    is_ctx = pl.program_id(1) == 0
    gain_ref[...] = g_ref[...] * (1.0 + jnp.where(is_ctx, modc_ref[1:2, :], modb_ref[1:2, :]))
    shift_ref[...] = jnp.where(is_ctx, modc_ref[0:1, :], modb_ref[0:1, :])
    for c in range(ROW_TILE // NM_CHUNK):
        rows = pl.ds(c * NM_CHUNK, NM_CHUNK)
        xv = x_ref[rows, :]
        inv = lax.rsqrt(jnp.mean(xv * xv, axis=-1, keepdims=True) + RMS_EPS)
        h_ref[rows, :] = (xv * inv * gain_ref[...] + shift_ref[...]).astype(BF16)
    for n in range(o_ref.shape[1] // NM_TN):
        cols = pl.ds(n * NM_TN, NM_TN)
        o_ref[:, cols] = jnp.dot(h_ref[...], w_ref[:, cols],
                                 preferred_element_type=F32).astype(BF16)


def _norm_mm(xc, mods, layer, g, w_bf, widx):
    n_out = w_bf.shape[-1]
    return pl.pallas_call(
        _norm_mm_kernel,
        grid=(B, T // ROW_TILE),
        in_specs=[
            pl.BlockSpec((None, ROW_TILE, D), lambda b, m: (b, m, 0)),
            pl.BlockSpec((None, None, 6, D), lambda b, m: (layer, b, 0, 0)),
            pl.BlockSpec((None, None, 6, D), lambda b, m: (layer, B, 0, 0)),
            pl.BlockSpec((None, 1, D), lambda b, m: (layer, 0, 0)),
            pl.BlockSpec((None, D, n_out), lambda b, m: (widx, 0, 0), pipeline_mode=pl.Buffered(1)),
        ],
        out_specs=pl.BlockSpec((None, ROW_TILE, n_out), lambda b, m: (b, m, 0)),
        out_shape=jax.ShapeDtypeStruct((B, T, n_out), BF16),
        scratch_shapes=[pltpu.VMEM((ROW_TILE, D), BF16),
                        pltpu.VMEM((1, D), F32), pltpu.VMEM((1, D), F32)],
        compiler_params=pltpu.CompilerParams(
            dimension_semantics=("arbitrary", "arbitrary"), vmem_limit_bytes=VMEM_LIMIT),
        name="norm_mm",
    )(xc, mods, mods, g, w_bf)


def _depthwise_taps(win, taps, width):
    nwin = CONV_ROWS + 2 * CONV_PAD
    acc = jnp.zeros((CONV_ROWS, win.shape[1]), F32)
    for res in range(8):
        offsets = [o for o in range(res, nwin - CONV_ROWS + 1, 8)
                   if 0 <= o - CONV_PAD + width // 2 < width]
        if not offsets:
            continue
        rolled = win if res == 0 else pltpu.roll(win, nwin - res, 0)
        for o in offsets:
            k = o - CONV_PAD + width // 2
            acc = acc + rolled[o - res:o - res + CONV_ROWS, :] * taps[k:k + 1, :]
    return acc


def _conv_kernel(bg_ref, cg_ref, xa_ref, ga_ref, gb_ref, ca_ref, dw_ref, dwb_ref,
                 ya_ref, zc_ref, vs_ref, zs_ref):
    ca = ca_ref[...]
    dw = dw_ref[...]
    dwb = dwb_ref[...]
    zeros_pad = jnp.zeros((CONV_PAD, CONV_CW), F32)

    def run_sequence(seq0, length):
        vs_ref[pl.ds(0, CONV_PAD), :] = zeros_pad
        zs_ref[pl.ds(0, CONV_PAD), :] = zeros_pad
        vs_ref[pl.ds(CONV_PAD + length, CONV_PAD), :] = zeros_pad
        zs_ref[pl.ds(CONV_PAD + length, CONV_PAD), :] = zeros_pad

        def fill(i, carry):
            r = pl.multiple_of(i * ROW_TILE, ROW_TILE)
            src = pl.ds(seq0 + r, ROW_TILE)
            dst = pl.ds(CONV_PAD + r, ROW_TILE)
            vs_ref[dst, :] = cg_ref[src, :].astype(F32) * xa_ref[src, :].astype(F32)
            zs_ref[dst, :] = ga_ref[src, :].astype(F32) * _sigmoid(gb_ref[src, :].astype(F32))
            return carry

        lax.fori_loop(0, length // ROW_TILE, fill, 0)

        def conv(i, carry):
            r = pl.multiple_of(i * CONV_ROWS, CONV_ROWS)
            win = pl.ds(r, CONV_ROWS + 2 * CONV_PAD)
            out = pl.ds(seq0 + r, CONV_ROWS)
            acc3 = _depthwise_taps(vs_ref[win, :], ca, SC_WIDTH)
            ya_ref[out, :] = (bg_ref[out, :].astype(F32) * acc3).astype(BF16)
            acc = _depthwise_taps(zs_ref[win, :], dw, CF_KERNEL)
            zc_ref[out, :] = (acc + dwb).astype(BF16)
            return carry

        lax.fori_loop(0, length // CONV_ROWS, conv, 0)

    run_sequence(0, CTX)
    run_sequence(CTX, SEQ)


def _conv(u, conv_a, dw_w, dw_b, j):
    nblk = SC_W // CONV_CW
    part = lambda k: pl.BlockSpec((None, T, CONV_CW), lambda b, cb: (b, 0, k * nblk + cb))
    out_spec = pl.BlockSpec((None, T, CONV_CW), lambda b, cb: (b, 0, cb))
    return pl.pallas_call(
        _conv_kernel,
        grid=(B, nblk),
        in_specs=[part(0), part(1), part(2), part(3), part(4),
                  pl.BlockSpec((None, SC_WIDTH, CONV_CW), lambda b, cb: (j, 0, cb)),
                  pl.BlockSpec((None, CF_KERNEL, CONV_CW), lambda b, cb: (j, 0, cb)),
                  pl.BlockSpec((None, 1, CONV_CW), lambda b, cb: (j, 0, cb))],
        out_specs=[out_spec, out_spec],
        out_shape=[jax.ShapeDtypeStruct((B, T, HALF), BF16)] * 2,
        scratch_shapes=[pltpu.VMEM((SEQ + 2 * CONV_PAD, CONV_CW), F32)] * 2,
        compiler_params=pltpu.CompilerParams(
            dimension_semantics=("arbitrary", "arbitrary"), vmem_limit_bytes=VMEM_LIMIT),
        name="conv_mixers",
    )(u, u, u, u, u, conv_a, dw_w, dw_b.reshape(dw_b.shape[0], 1, dw_b.shape[1]))


N_RPB_DR = 2 * NA_ROWS - 1
N_RPB_DC = 2 * NA_COLS - 1
ATT_MASKED = N_RPB_DR


def _att_block_geometry(blk):
    r0 = blk * ATT_QROWS
    return min(max(r0 - NA_ROWS // 2, 0), GRID_ROWS - ATT_KROWS)


def _softmax_pv(parts):
    mx = None
    for s, _ in parts:
        pm = jnp.max(s, axis=-1, keepdims=True)
        mx = pm if mx is None else jnp.maximum(mx, pm)
    den = None
    num = None
    for s, v in parts:
        e = jnp.exp(s - mx)
        ps = jnp.sum(e, axis=-1, keepdims=True)
        pv = jnp.dot(e.astype(BF16), v, preferred_element_type=F32)
        den = ps if den is None else den + ps
        num = pv if num is None else num + pv
    return num / den


def _att_kernel(rpb_ref, q_ref, k_ref, v_ref, o_ref, tile_ref, *, rpb_base):
    @pl.when(pl.program_id(1) == 0)
    def _():
        base = (rpb_base + pl.program_id(0)) * (N_RPB_DR * N_RPB_DC)
        ci = lax.broadcasted_iota(I32, (GRID_W, 2 * GRID_W), 0)
        wi = lax.broadcasted_iota(I32, (GRID_W, 2 * GRID_W), 1) & (GRID_W - 1)
        cs = jnp.clip(ci - NA_COLS // 2, 0, GRID_W - NA_COLS)
        dsel = jnp.where((wi >= cs) & (wi < cs + NA_COLS), wi - ci + (NA_COLS - 1), -1)
        masked = jnp.full((GRID_W, 2 * GRID_W), NEG_INF, F32)
        for dr in range(N_RPB_DR):
            t = masked
            for dc in range(N_RPB_DC):
                t = jnp.where(dsel == dc, rpb_ref[base + dr * N_RPB_DC + dc], t)
            tile_ref[dr] = t
        tile_ref[ATT_MASKED] = masked

    def block_bias(blk, kr0):
        low_half = lax.broadcasted_iota(I32, (GRID_W, 2 * GRID_W), 1) < GRID_W
        bands = []
        for qr in range(ATT_QROWS):
            r = blk * ATT_QROWS + qr
            rs = min(max(r - NA_ROWS // 2, 0), GRID_ROWS - NA_ROWS)
            tiles = []
            for jp in range(ATT_KROWS // 2):
                idx = [kr - r + (NA_ROWS - 1) if rs <= kr < rs + NA_ROWS else ATT_MASKED
                       for kr in (kr0 + 2 * jp, kr0 + 2 * jp + 1)]
                if idx[0] == idx[1]:
                    tiles.append(tile_ref[idx[0]])
                else:
                    tiles.append(jnp.where(low_half, tile_ref[idx[0]], tile_ref[idx[1]]))
            bands.append(jnp.concatenate(tiles, axis=1))
        return jnp.concatenate(bands, axis=0)

    scale = NA_DH ** -0.5
    nt = (((1,), (1,)), ((), ()))
    kc = k_ref[pl.ds(0, CTX), :]
    vc = v_ref[pl.ds(0, CTX), :]

    qc = q_ref[pl.ds(0, CTX), :]
    s_cc = lax.dot_general(qc, kc, nt, preferred_element_type=F32) * scale
    o_ref[pl.ds(0, CTX), :] = _softmax_pv([(s_cc, vc)]).astype(BF16)

    for blk in range(GRID_ROWS // ATT_QROWS):
        kr0 = _att_block_geometry(blk)
        qrows = pl.ds(CTX + blk * ATT_Q, ATT_Q)
        krows = pl.ds(CTX + kr0 * GRID_W, ATT_K)
        qb = q_ref[qrows, :]
        s_w = lax.dot_general(qb, k_ref[krows, :], nt, preferred_element_type=F32) * scale
        s_w = s_w + block_bias(blk, kr0)
        s_c = lax.dot_general(qb, kc, nt, preferred_element_type=F32) * scale
        o_ref[qrows, :] = _softmax_pv([(s_w, v_ref[krows, :]), (s_c, vc)]).astype(BF16)


def _attention(u, rpb_flat, j):
    col = lambda off: pl.BlockSpec((None, T, NA_DH), lambda h, b: (b, 0, off + h))
    return pl.pallas_call(
        functools.partial(_att_kernel, rpb_base=j * NA_HEADS),
        grid=(NA_HEADS, B),
        in_specs=[pl.BlockSpec(memory_space=pltpu.SMEM),
                  col(0), col(NA_HEADS), col(2 * NA_HEADS)],
        out_specs=pl.BlockSpec((None, T, NA_DH), lambda h, b: (b, 0, h)),
        out_shape=jax.ShapeDtypeStruct((B, T, HALF), BF16),
        scratch_shapes=[pltpu.VMEM((N_RPB_DR + 1, GRID_W, 2 * GRID_W), F32)],
        compiler_params=pltpu.CompilerParams(
            dimension_semantics=("arbitrary", "arbitrary"), vmem_limit_bytes=VMEM_LIMIT),
        name="neighbourhood_attention",
    )(rpb_flat, u, u, u)


POOL_KWIN = 512


def _pool_kernel(p_ref, pw_ref, ps_ref, o_ref):
    half = lax.shift_left(jnp.int32(1), pl.program_id(1))
    pw = pw_ref[...].astype(BF16)
    sc = ps_ref[...]

    def tile(row0, seq0, length, ks, kwin):
        t = (row0 - seq0) + lax.broadcasted_iota(I32, (ROW_TILE, 1), 0)
        s = ks + lax.broadcasted_iota(I32, (1, kwin), 1)
        lo = t - half
        hi = t + half - 1
        band = jnp.where((s >= lo) & (s <= hi), 1.0, 0.0).astype(BF16)
        ssum = jnp.dot(band, p_ref[pl.ds(seq0 + ks, kwin), :], preferred_element_type=F32)
        cnt = (jnp.minimum(hi, length - 1) - jnp.maximum(lo, 0) + 1).astype(F32)
        mix = ssum / cnt - p_ref[pl.ds(row0, ROW_TILE), :].astype(F32)
        y = jnp.dot(mix.astype(BF16), pw, preferred_element_type=F32) * sc
        o_ref[pl.ds(row0, ROW_TILE), :] = y.astype(BF16)

    tile(0, 0, CTX, 0, CTX)
    for jt in range(SEQ // ROW_TILE):
        ks = min(max(ROW_TILE * jt - (POOL_KWIN - ROW_TILE) // 2, 0), SEQ - POOL_KWIN)
        tile(CTX + ROW_TILE * jt, CTX, SEQ, ks, POOL_KWIN)


def _pool(u, pool_w, pool_scale, j):
    p_block0 = 3 * HALF // POOL_GC
    return pl.pallas_call(
        _pool_kernel,
        grid=(B, N_POOL),
        in_specs=[pl.BlockSpec((None, T, POOL_GC), lambda b, g: (b, 0, p_block0 + g)),
                  pl.BlockSpec((None, None, POOL_GC, POOL_GC), lambda b, g: (j, g, 0, 0)),
                  pl.BlockSpec((None, None, 1, POOL_GC), lambda b, g: (j, g, 0, 0))],
        out_specs=pl.BlockSpec((None, T, POOL_GC), lambda b, g: (b, 0, g)),
        out_shape=jax.ShapeDtypeStruct((B, T, HALF), BF16),
        compiler_params=pltpu.CompilerParams(
            dimension_semantics=("arbitrary", "arbitrary"), vmem_limit_bytes=VMEM_LIMIT),
        name="pool_mixer",
    )(u, pool_w, pool_scale.reshape(pool_scale.shape[0], N_POOL, 1, POOL_GC))


ROUTE_LANES = 128


def _route(logits):
    lane = lax.broadcasted_iota(I32, logits.shape, 1)
    lanef = lane.astype(F32)
    is_g = lane < N_GROUPS
    gl = jnp.where(is_g, logits, NEG_INF)
    ge = jnp.where(is_g, jnp.exp(gl - jnp.max(gl, axis=-1, keepdims=True)), 0.0)
    gp = ge / jnp.sum(ge, axis=-1, keepdims=True)
    g_p = jnp.max(gp, axis=-1, keepdims=True)
    g_idx = jnp.min(jnp.where(is_g & (gp == g_p), lanef, float(ROUTE_LANES)), axis=-1, keepdims=True)
    first = float(N_GROUPS) + float(EPG) * g_idx
    sel = (lanef >= first) & (lanef < first + float(EPG))
    el = jnp.where(sel, logits, NEG_INF)
    ee = jnp.where(sel, jnp.exp(el - jnp.max(el, axis=-1, keepdims=True)), 0.0)
    ep = jnp.where(sel, ee / jnp.sum(ee, axis=-1, keepdims=True), -1.0)
    p1 = jnp.max(ep, axis=-1, keepdims=True)
    i1 = jnp.min(jnp.where(ep == p1, lanef, float(ROUTE_LANES)), axis=-1, keepdims=True)
    ep2 = jnp.where(lanef == i1, -1.0, ep)
    p2 = jnp.max(ep2, axis=-1, keepdims=True)
    i2 = jnp.min(jnp.where(ep2 == p2, lanef, float(ROUTE_LANES)), axis=-1, keepdims=True)
    psum = p1 + p2
    gate1 = g_p * p1 / psum
    gate2 = g_p * p2 / psum
    return jnp.where(lane == 0, i1 - float(N_GROUPS),
                     jnp.where(lane == 1, i2 - float(N_GROUPS),
                               jnp.where(lane == 2, gate1,
                                         jnp.where(lane == 3, gate2, 0.0))))


def _proj_kernel(ya_ref, yb_ref, w_ref, lng_ref, lnb_ref, x_ref, modb_ref, modc_ref, g2_ref,
                 rwh_ref, rwl_ref, rb_ref, xo_ref, hp_ref, ri_ref,
                 yb_s, out_s, hh_s, hl_s, vec_s, *, even):
    nchunk = ROW_TILE // NM_CHUNK

    def rows_of(c):
        return pl.ds(c * NM_CHUNK, NM_CHUNK)

    if even:
        def layernorm_swish(c, carry):
            rows = rows_of(c)
            zc = yb_ref[rows, :].astype(F32)
            dz = zc - jnp.mean(zc, axis=-1, keepdims=True)
            var = jnp.mean(dz * dz, axis=-1, keepdims=True)
            z = dz * lax.rsqrt(var + LN_EPS) * lng_ref[...] + lnb_ref[...]
            yb_s[rows, :] = (z * _sigmoid(z)).astype(BF16)
            return carry

        for c in range(nchunk):
            layernorm_swish(c, 0)
        yb = yb_s[...]
    else:
        yb = yb_ref[...]
    out_s[...] = (jnp.dot(ya_ref[...], w_ref[pl.ds(0, HALF), :], preferred_element_type=F32)
                  + jnp.dot(yb, w_ref[pl.ds(HALF, HALF), :], preferred_element_type=F32))

    mod = jnp.where(pl.program_id(1) == 0, modc_ref[...], modb_ref[...])
    vec_s[0:1, :] = mod[2:3, :]
    vec_s[1:2, :] = g2_ref[...] * (1.0 + mod[4:5, :])
    vec_s[2:3, :] = mod[3:4, :]

    def residual_norm(c, carry):
        rows = rows_of(c)
        xn = x_ref[rows, :] + vec_s[0:1, :] * out_s[rows, :]
        xo_ref[rows, :] = xn
        inv = lax.rsqrt(jnp.mean(xn * xn, axis=-1, keepdims=True) + RMS_EPS)
        h2 = xn * inv * vec_s[1:2, :] + vec_s[2:3, :]
        hp_ref[rows, :] = h2
        hb = h2.astype(BF16)
        hh_s[rows, :] = hb
        hl_s[rows, :] = (h2 - hb.astype(F32)).astype(BF16)
        return carry

    for c in range(nchunk):
        residual_norm(c, 0)

    hh = hh_s[...]
    logits = (jnp.dot(hh, rwh_ref[...], preferred_element_type=F32)
              + jnp.dot(hl_s[...], rwh_ref[...], preferred_element_type=F32)
              + jnp.dot(hh, rwl_ref[...], preferred_element_type=F32)) + rb_ref[...]
    ri_ref[...] = _route(logits)


def _proj(ya, yb, w_bf, widx, ln_g, ln_b, xc, mods, layer, g2, rw_hi, rw_lo, rb, even):
    row = lambda width: pl.BlockSpec((None, ROW_TILE, width), lambda b, m: (b, m, 0))
    return pl.pallas_call(
        functools.partial(_proj_kernel, even=even),
        grid=(B, T // ROW_TILE),
        in_specs=[row(HALF), row(HALF),
                  pl.BlockSpec((None, D, D), lambda b, m: (widx, 0, 0)),
                  pl.BlockSpec((None, 1, HALF), lambda b, m: (widx, 0, 0)),
                  pl.BlockSpec((None, 1, HALF), lambda b, m: (widx, 0, 0)),
                  row(D),
                  pl.BlockSpec((None, None, 6, D), lambda b, m: (layer, b, 0, 0)),
                  pl.BlockSpec((None, None, 6, D), lambda b, m: (layer, B, 0, 0)),
                  pl.BlockSpec((None, 1, D), lambda b, m: (layer, 0, 0)),
                  pl.BlockSpec((None, D, ROUTE_LANES), lambda b, m: (layer, 0, 0)),
                  pl.BlockSpec((None, D, ROUTE_LANES), lambda b, m: (layer, 0, 0)),
                  pl.BlockSpec((None, 1, ROUTE_LANES), lambda b, m: (layer, 0, 0))],
        out_specs=[row(D), row(D), row(ROUTE_LANES)],
        out_shape=[jax.ShapeDtypeStruct((B, T, D), F32),
                   jax.ShapeDtypeStruct((B, T, D), F32),
                   jax.ShapeDtypeStruct((B, T, ROUTE_LANES), F32)],
        scratch_shapes=[pltpu.VMEM((ROW_TILE, HALF), BF16),
                        pltpu.VMEM((ROW_TILE, D), F32),
                        pltpu.VMEM((ROW_TILE, D), BF16),
                        pltpu.VMEM((ROW_TILE, D), BF16),
                        pltpu.VMEM((8, D), F32)],
        compiler_params=pltpu.CompilerParams(
            dimension_semantics=("arbitrary", "arbitrary"), vmem_limit_bytes=VMEM_LIMIT),
        name="out_proj_router",
    )(ya, yb, w_bf, ln_g, ln_b, xc, mods, mods, g2, rw_hi, rw_lo, rb)


def _positions_kernel(ri_ref, pos_ref, base_ref):
    ntiles = NTOK // ROW_TILE
    lanef = lax.broadcasted_iota(I32, (ROW_TILE, ROUTE_LANES), 1).astype(F32)

    def count(i, acc):
        r = ri_ref[pl.ds(pl.multiple_of(i * ROW_TILE, ROW_TILE), ROW_TILE), :]
        hit = (lanef == r[:, 0:1]) | (lanef == r[:, 1:2])
        return acc + jnp.sum(jnp.where(hit, 1.0, 0.0), axis=0, keepdims=True)

    cnt = lax.fori_loop(0, ntiles, count, jnp.zeros((1, ROUTE_LANES), F32))
    ri = lax.broadcasted_iota(I32, (ROUTE_LANES, ROUTE_LANES), 0)
    ci = lax.broadcasted_iota(I32, (ROUTE_LANES, ROUTE_LANES), 1)
    before = jnp.where(ri < ci, 1.0, 0.0)
    cnt8 = jnp.broadcast_to(cnt, (8, ROUTE_LANES))
    base8 = jnp.dot(cnt8, before, preferred_element_type=F32, precision=HIGHEST)
    base_ref[...] = base8.astype(I32)
    base_col = jnp.broadcast_to(base8[0:1, :], (ROUTE_LANES, ROUTE_LANES)).T[:, 0:1]

    ti = lax.broadcasted_iota(I32, (ROW_TILE, ROW_TILE), 0)
    tj = lax.broadcasted_iota(I32, (ROW_TILE, ROW_TILE), 1)
    earlier = jnp.where(ti < tj, 1.0, 0.0).astype(BF16)
    subf = lax.broadcasted_iota(I32, (ROUTE_LANES, ROW_TILE), 0).astype(F32)

    def place(i, carry):
        start = pl.multiple_of(i * ROW_TILE, ROW_TILE)
        rt = ri_ref[pl.ds(start, ROW_TILE), :].T
        e0 = rt[0:1, :]
        e1 = rt[1:2, :]
        hit0 = subf == e0
        hit1 = subf == e1
        hits = jnp.where(hit0 | hit1, 1.0, 0.0)
        rank = jnp.dot(hits.astype(BF16), earlier, preferred_element_type=F32) + carry + base_col
        pos_ref[0:1, pl.ds(start, ROW_TILE)] = jnp.sum(
            jnp.where(hit0, rank, 0.0), axis=0, keepdims=True).astype(I32)
        pos_ref[1:2, pl.ds(start, ROW_TILE)] = jnp.sum(
            jnp.where(hit1, rank, 0.0), axis=0, keepdims=True).astype(I32)
        return carry + jnp.sum(hits, axis=1, keepdims=True)

    lax.fori_loop(0, ntiles, place, jnp.zeros((ROUTE_LANES, 1), F32))


def _positions(ri_flat):
    return pl.pallas_call(
        _positions_kernel,
        out_shape=[jax.ShapeDtypeStruct((2, NTOK), I32),
                   jax.ShapeDtypeStruct((8, ROUTE_LANES), I32)],
        compiler_params=pltpu.CompilerParams(vmem_limit_bytes=VMEM_LIMIT),
        name="moe_positions",
    )(ri_flat)


INVERT_UNROLL = 8


def _items_kernel(base_ref, pos_ref, it_ref, src_ref):
    def init(i, carry):
        it_ref[ITEM_NEW + i] = 0
        it_ref[ITEM_SLOT + i] = 0
        it_ref[ITEM_NEXT + i] = -1
        return carry

    lax.fori_loop(0, N_ITEMS, init, 0)

    def per_expert(e, carry):
        n, run, run_start = carry
        s = base_ref[e]
        t = base_ref[e + 1]
        first = lax.shift_right_logical(s, 8)
        last = lax.shift_right_logical(jnp.maximum(t - 1, 0), 8)
        used = t > s
        ntile = jnp.where(used, last - first + 1, 0)

        def per_tile(k, n2):
            tile = first + k
            it_ref[n2] = tile
            it_ref[N_ITEMS + n2] = e
            it_ref[2 * N_ITEMS + n2] = jnp.maximum(s - tile * ROW_TILE, 0)
            it_ref[3 * N_ITEMS + n2] = jnp.minimum(t - tile * ROW_TILE, ROW_TILE)
            return n2 + 1

        n_end = lax.fori_loop(0, ntile, per_tile, n)
        it_ref[ITEM_NEW + n] = jnp.where(used, 1, it_ref[ITEM_NEW + n])
        it_ref[ITEM_SLOT + n] = jnp.where(used, run & 1, it_ref[ITEM_SLOT + n])
        link = used & (run > 0)
        it_ref[ITEM_NEXT + run_start] = jnp.where(link, e, it_ref[ITEM_NEXT + run_start])
        return n_end, run + jnp.where(used, 1, 0), jnp.where(used, n, run_start)

    n_used, _, _ = lax.fori_loop(0, N_EXPERTS, per_expert,
                                 (jnp.int32(0), jnp.int32(0), jnp.int32(0)))
    last_e = it_ref[N_ITEMS + n_used - 1]

    def pad(i, carry):
        it_ref[i] = N_SORT_TILES - 1
        it_ref[N_ITEMS + i] = last_e
        it_ref[2 * N_ITEMS + i] = 0
        it_ref[3 * N_ITEMS + i] = 0
        return carry

    lax.fori_loop(n_used, N_ITEMS, pad, 0)

    def invert(g, carry):
        for k in range(INVERT_UNROLL):
            t = g * INVERT_UNROLL + k
            src_ref[pos_ref[t]] = t
            src_ref[pos_ref[NTOK + t]] = t
        return carry

    lax.fori_loop(0, NTOK // INVERT_UNROLL, invert, 0)


def _items(base_flat, pos_flat):
    smem = pl.BlockSpec(memory_space=pltpu.SMEM)
    return pl.pallas_call(
        _items_kernel,
        in_specs=[smem, smem],
        out_specs=[smem, smem],
        out_shape=[jax.ShapeDtypeStruct((7 * N_ITEMS,), I32),
                   jax.ShapeDtypeStruct((NSLOT,), I32)],
        name="moe_items",
    )(base_flat, pos_flat)


ROW_DMA_UNROLL = 8


def _gmm_kernel(it_ref, src_ref, h_hbm, w1_hbm, w3_hbm, w2_hbm, o_ref,
                xbuf, w1_raw, w3_raw, w2_raw, w1_bf, w3_bf, w2_bf, sems, row_sems, *, layer):
    i = pl.program_id(0)
    tile = it_ref[i]
    par = tile & 1

    def gather_rows(tl, p):
        def rows(g, carry):
            for k in range(ROW_DMA_UNROLL):
                r = g * ROW_DMA_UNROLL + k
                pltpu.make_async_copy(h_hbm.at[pl.ds(src_ref[tl * ROW_TILE + r], 1)],
                                      xbuf.at[p, pl.ds(r, 1)], row_sems.at[p]).start()
            return carry

        lax.fori_loop(0, ROW_TILE // ROW_DMA_UNROLL, rows, 0)
    expert = it_ref[N_ITEMS + i]
    lo = it_ref[2 * N_ITEMS + i]
    hi = it_ref[3 * N_ITEMS + i]
    slot = it_ref[ITEM_SLOT + i]
    nxt = it_ref[ITEM_NEXT + i]
    prev_tile = it_ref[jnp.maximum(i - 1, 0)]

    def weight_copies(e, s):
        return (pltpu.make_async_copy(w1_hbm.at[layer, e], w1_raw.at[s], sems.at[s, 0]),
                pltpu.make_async_copy(w3_hbm.at[layer, e], w3_raw.at[s], sems.at[s, 1]),
                pltpu.make_async_copy(w2_hbm.at[layer, e], w2_raw.at[s], sems.at[s, 2]))

    @pl.when(it_ref[ITEM_NEW + i] == 1)
    def _():
        @pl.when(i == 0)
        def _():
            for cp in weight_copies(expert, slot):
                cp.start()

        for cp in weight_copies(expert, slot):
            cp.wait()

        @pl.when(nxt >= 0)
        def _():
            for cp in weight_copies(nxt, 1 - slot):
                cp.start()

        w1_bf[...] = w1_raw[slot].astype(BF16)
        w3_bf[...] = w3_raw[slot].astype(BF16)
        w2_bf[...] = w2_raw[slot].astype(BF16)

    @pl.when((i == 0) | (tile != prev_tile))
    def _():
        @pl.when(i == 0)
        def _():
            gather_rows(tile, par)

        pltpu.make_async_copy(xbuf.at[par], xbuf.at[par], row_sems.at[par]).wait()

        @pl.when(tile + 1 < N_SORT_TILES)
        def _():
            gather_rows(tile + 1, 1 - par)

        o_ref[...] = jnp.zeros_like(o_ref)

    @pl.when(hi > lo)
    def _():
        xb = xbuf[par].astype(BF16)
        a = jnp.dot(xb, w1_bf[...], preferred_element_type=F32)
        b = jnp.dot(xb, w3_bf[...], preferred_element_type=F32)
        rows = lax.broadcasted_iota(I32, (ROW_TILE, 1), 0)
        hid = jnp.where((rows >= lo) & (rows < hi), a * _sigmoid(a) * b, 0.0)
        o_ref[...] += jnp.dot(hid.astype(BF16), w2_bf[...], preferred_element_type=F32)


def _gmm(items, src, h_flat, w1, w3, w2, layer):
    hbm = pl.BlockSpec(memory_space=pl.ANY)
    return pl.pallas_call(
        functools.partial(_gmm_kernel, layer=layer),
        grid_spec=pltpu.PrefetchScalarGridSpec(
            num_scalar_prefetch=2,
            grid=(N_ITEMS,),
            in_specs=[hbm, hbm, hbm, hbm],
            out_specs=pl.BlockSpec((ROW_TILE, D), lambda i, it, src: (it[i], 0)),
            scratch_shapes=[pltpu.VMEM((2, ROW_TILE, D), F32),
                            pltpu.VMEM((2, D, EXPERT_FF), F32),
                            pltpu.VMEM((2, D, EXPERT_FF), F32),
                            pltpu.VMEM((2, EXPERT_FF, D), F32),
                            pltpu.VMEM((D, EXPERT_FF), BF16),
                            pltpu.VMEM((D, EXPERT_FF), BF16),
                            pltpu.VMEM((EXPERT_FF, D), BF16),
                            pltpu.SemaphoreType.DMA((2, 3)),
                            pltpu.SemaphoreType.DMA((2,))],
        ),
        out_shape=jax.ShapeDtypeStruct((NSLOT, D), F32),
        compiler_params=pltpu.CompilerParams(
            dimension_semantics=("arbitrary",), vmem_limit_bytes=VMEM_LIMIT),
        name="moe_experts",
    )(items, src, h_flat, w1, w3, w2)


def _combine_kernel(pos_ref, ys_hbm, x_ref, ri_ref, modb_ref, modc_ref, fg_ref, o_ref, buf, gate_s, sems,
                    *, final, tiles_per_b, m_off):
    i = pl.program_id(0)
    ntiles = pl.num_programs(0)

    def issue(tile_i, par):
        t0 = (tile_i // tiles_per_b) * T + (tile_i % tiles_per_b + m_off) * ROW_TILE

        def rows(g, carry):
            for k in range(ROW_DMA_UNROLL):
                r = g * ROW_DMA_UNROLL + k
                for s in range(2):
                    pltpu.make_async_copy(ys_hbm.at[pl.ds(pos_ref[s * NTOK + t0 + r], 1)],
                                          buf.at[par, s, pl.ds(r, 1)], sems.at[par]).start()
            return carry

        lax.fori_loop(0, ROW_TILE // ROW_DMA_UNROLL, rows, 0)

    @pl.when(i == 0)
    def _():
        issue(0, 0)

    @pl.when(i + 1 < ntiles)
    def _():
        issue(i + 1, (i + 1) % 2)

    par = i % 2
    pltpu.make_async_copy(buf.at[par], buf.at[par], sems.at[par]).wait()
    m = i % tiles_per_b + m_off
    gate_s[...] = jnp.where(m == 0, modc_ref[5:6, :], modb_ref[5:6, :])
    for c in range(ROW_TILE // NM_CHUNK):
        rows = pl.ds(c * NM_CHUNK, NM_CHUNK)
        g0 = ri_ref[rows, 2:3]
        g1 = ri_ref[rows, 3:4]
        xn = x_ref[rows, :] + gate_s[...] * (g0 * buf[par, 0, rows, :] + g1 * buf[par, 1, rows, :])
        if final:
            inv = lax.rsqrt(jnp.mean(xn * xn, axis=-1, keepdims=True) + RMS_EPS)
            xn = xn * inv * fg_ref[...]
        o_ref[rows, :] = xn


def _combine(pos_flat, ys, xc, ri, mods, layer, final_g, final):
    if final:
        tiles_per_b, m_off, rows_out = SEQ // ROW_TILE, CTX // ROW_TILE, SEQ
    else:
        tiles_per_b, m_off, rows_out = T // ROW_TILE, 0, T
    tok = lambda width: pl.BlockSpec(
        (None, ROW_TILE, width), lambda i, pos: (i // tiles_per_b, i % tiles_per_b + m_off, 0))
    return pl.pallas_call(
        functools.partial(_combine_kernel, final=final, tiles_per_b=tiles_per_b, m_off=m_off),
        grid_spec=pltpu.PrefetchScalarGridSpec(
            num_scalar_prefetch=1,
            grid=(B * tiles_per_b,),
            in_specs=[
                pl.BlockSpec(memory_space=pl.ANY),
                tok(D),
                tok(ROUTE_LANES),
                pl.BlockSpec((None, None, 6, D), lambda i, pos: (layer, i // tiles_per_b, 0, 0)),
                pl.BlockSpec((None, None, 6, D), lambda i, pos: (layer, B, 0, 0)),
                pl.BlockSpec((1, D), lambda i, pos: (0, 0)),
            ],
            out_specs=pl.BlockSpec((None, ROW_TILE, D),
                                   lambda i, pos: (i // tiles_per_b, i % tiles_per_b, 0)),
            scratch_shapes=[pltpu.VMEM((2, 2, ROW_TILE, D), F32),
                            pltpu.VMEM((1, D), F32),
                            pltpu.SemaphoreType.DMA((2,))],
        ),
        out_shape=jax.ShapeDtypeStruct((B, rows_out, D), F32),
        compiler_params=pltpu.CompilerParams(
            dimension_semantics=("arbitrary",), vmem_limit_bytes=VMEM_LIMIT),
        name="moe_combine",
    )(pos_flat, ys, xc, ri, mods, mods, final_g)


def kernel(x, c, ctx, c_ctx, ada_w, ada_b, norm1_g, norm2_g, ev_w_in, ev_conv_a, ev_dw_w, ev_dw_b,
           ev_ln_g, ev_ln_b, ev_w_out, od_w_in, od_rpb, od_pool_w, od_pool_scale, od_w_out,
           moe_rg_w, moe_rg_b, moe_re_w, moe_re_b, moe_w1, moe_w3, moe_w2, final_g):
    xc = jnp.concatenate([ctx, x], axis=1)
    c8 = jnp.concatenate([c, c_ctx[None, :], jnp.zeros((8 - B - 1, D), F32)], axis=0)
    mods = _ada(c8, ada_w, ada_b).reshape(DEPTH, 8, 6, D)

    ev_w_in_bf = ev_w_in.astype(BF16)
    od_w_in_bf = od_w_in.astype(BF16)
    ev_w_out_bf = ev_w_out.astype(BF16)
    od_w_out_bf = od_w_out.astype(BF16)
    pad = ROUTE_LANES - N_GROUPS - N_EXPERTS
    rw = jnp.concatenate([moe_rg_w, moe_re_w, jnp.zeros((DEPTH, D, pad), F32)], axis=-1)
    rw_hi = rw.astype(BF16)
    rw_lo = (rw - rw_hi.astype(F32)).astype(BF16)
    rb = jnp.concatenate([moe_rg_b, moe_re_b, jnp.zeros((DEPTH, pad), F32)],
                         axis=-1).reshape(DEPTH, 1, ROUTE_LANES)
    final_g2 = final_g.reshape(1, D)
    norm1_g = norm1_g.reshape(DEPTH, 1, D)
    norm2_g = norm2_g.reshape(DEPTH, 1, D)
    ev_ln_g = ev_ln_g.reshape(-1, 1, HALF)
    ev_ln_b = ev_ln_b.reshape(-1, 1, HALF)

    for i in range(DEPTH):
        j = i // 2
        last = i == DEPTH - 1
        if i % 2 == 1:
            u = _norm_mm(xc, mods, i, norm1_g, od_w_in_bf, j)
            ya = _attention(u, od_rpb.reshape(-1), j)
            yb = _pool(u, od_pool_w, od_pool_scale, j)
            xc, hp, ri = _proj(ya, yb, od_w_out_bf, j, ev_ln_g, ev_ln_b, xc, mods, i, norm2_g,
                               rw_hi, rw_lo, rb, even=False)
        else:
            u = _norm_mm(xc, mods, i, norm1_g, ev_w_in_bf, j)
            ya, yb = _conv(u, ev_conv_a, ev_dw_w, ev_dw_b, j)
            xc, hp, ri = _proj(ya, yb, ev_w_out_bf, j, ev_ln_g, ev_ln_b, xc, mods, i, norm2_g,
                               rw_hi, rw_lo, rb, even=True)
        pos, base = _positions(ri.reshape(NTOK, ROUTE_LANES))
        pos_flat = pos.reshape(2 * NTOK)
        items, src = _items(base[0, :N_EXPERTS + 1 + 7], pos_flat)
        ys = _gmm(items, src, hp.reshape(NTOK, D), moe_w1, moe_w3, moe_w2, i)
        xc = _combine(pos_flat, ys, xc, ri, mods, i, final_g2, last)
    return xc
```

```python
import functools

import numpy as np
import jax
import jax.numpy as jnp
from jax import lax
from jax.experimental import pallas as pl
from jax.experimental.pallas import tpu as pltpu

F32 = jnp.float32
BF16 = jnp.bfloat16
I32 = jnp.int32
HIGHEST = lax.Precision.HIGHEST

D = 2048
B = 4
SEQ = 2048
CTX = 256
T = CTX + SEQ
NTOK = B * T
DEPTH = 4
GRID_W = 64
GRID_ROWS = SEQ // GRID_W
HALF = D // 2
SC_W = 1024
CF_KERNEL = 31
SC_WIDTH = 3
NA_HEADS = 8
NA_DH = 128
NA_ROWS = 8
NA_COLS = 16
N_POOL = 4
POOL_GC = 256
N_EXPERTS = 32
N_GROUPS = 4
EPG = 8
EXPERT_FF = 256
RMS_EPS = 1e-6
LN_EPS = 1e-5
NEG_INF = -1e30

ROW_TILE = 256
NSLOT = 2 * NTOK
N_SORT_TILES = NSLOT // ROW_TILE
N_ITEMS = 128
ITEM_NEW, ITEM_SLOT, ITEM_NEXT = 4 * N_ITEMS, 5 * N_ITEMS, 6 * N_ITEMS
NM_TN = 1024
NM_CHUNK = 16
ATT_QROWS = 4
ATT_Q = ATT_QROWS * GRID_W
ATT_KROWS = 12
ATT_K = ATT_KROWS * GRID_W
CONV_CW = 256
CONV_PAD = 16
CONV_ROWS = 32
VMEM_LIMIT = 56 * 1024 * 1024


def _sigmoid(v):
    return 1.0 / (1.0 + jnp.exp(-v))


ADA_TN = 1024


def _ada_kernel(c_ref, w_ref, b_ref, o_ref):
    cv = c_ref[...]
    s = cv * _sigmoid(cv)
    s_hi = s.astype(BF16)
    s_lo = (s - s_hi.astype(F32)).astype(BF16)
    w = w_ref[0]
    w_hi = w.astype(BF16)
    w_lo = (w - w_hi.astype(F32)).astype(BF16)
    o_ref[0] = (jnp.dot(s_hi, w_hi, preferred_element_type=F32)
                + jnp.dot(s_lo, w_hi, preferred_element_type=F32)
                + jnp.dot(s_hi, w_lo, preferred_element_type=F32)) + b_ref[0]


def _ada(c8, ada_w, ada_b):
    n = ada_w.shape[-1]
    return pl.pallas_call(
        _ada_kernel,
        grid=(DEPTH, n // ADA_TN),
        in_specs=[
            pl.BlockSpec((8, D), lambda l, j: (0, 0)),
            pl.BlockSpec((1, D, ADA_TN), lambda l, j: (l, 0, j)),
            pl.BlockSpec((1, 1, ADA_TN), lambda l, j: (l, 0, j)),
        ],
        out_specs=pl.BlockSpec((1, 8, ADA_TN), lambda l, j: (l, 0, j)),
        out_shape=jax.ShapeDtypeStruct((DEPTH, 8, n), F32),
        compiler_params=pltpu.CompilerParams(
            dimension_semantics=("arbitrary", "arbitrary"), vmem_limit_bytes=VMEM_LIMIT),
        name="ada",
    )(c8, ada_w, ada_b.reshape(DEPTH, 1, n))


def _norm_mm_kernel(x_ref, modb_ref, modc_ref, g_ref, w_ref, o_ref, h_ref, gain_ref, shift_ref):
    is_ctx = pl.program_id(1) == 0
    gain_ref[...] = g_ref[...] * (1.0 + jnp.where(is_ctx, modc_ref[1:2, :], modb_ref[1:2, :]))
    shift_ref[...] = jnp.where(is_ctx, modc_ref[0:1, :], modb_ref[0:1, :])
    for c in range(ROW_TILE // NM_CHUNK):
        rows = pl.ds(c * NM_CHUNK, NM_CHUNK)
        xv = x_ref[rows, :]
        inv = lax.rsqrt(jnp.mean(xv * xv, axis=-1, keepdims=True) + RMS_EPS)
        h_ref[rows, :] = (xv * inv * gain_ref[...] + shift_ref[...]).astype(BF16)
    for n in range(o_ref.shape[1] // NM_TN):
        cols = pl.ds(n * NM_TN, NM_TN)
        o_ref[:, cols] = jnp.dot(h_ref[...], w_ref[:, cols],
                                 preferred_element_type=F32).astype(BF16)


def _norm_mm(xc, mods, layer, g, w_bf, widx):
    n_out = w_bf.shape[-1]
    return pl.pallas_call(
        _norm_mm_kernel,
        grid=(B, T // ROW_TILE),
        in_specs=[
            pl.BlockSpec((None, ROW_TILE, D), lambda b, m: (b, m, 0)),
            pl.BlockSpec((None, None, 6, D), lambda b, m: (layer, b, 0, 0)),
            pl.BlockSpec((None, None, 6, D), lambda b, m: (layer, B, 0, 0)),
            pl.BlockSpec((None, 1, D), lambda b, m: (layer, 0, 0)),
            pl.BlockSpec((None, D, n_out), lambda b, m: (widx, 0, 0), pipeline_mode=pl.Buffered(1)),
        ],
        out_specs=pl.BlockSpec((None, ROW_TILE, n_out), lambda b, m: (b, m, 0)),
        out_shape=jax.ShapeDtypeStruct((B, T, n_out), BF16),
        scratch_shapes=[pltpu.VMEM((ROW_TILE, D), BF16),
                        pltpu.VMEM((1, D), F32), pltpu.VMEM((1, D), F32)],
        compiler_params=pltpu.CompilerParams(
            dimension_semantics=("arbitrary", "arbitrary"), vmem_limit_bytes=VMEM_LIMIT),
        name="norm_mm",
    )(xc, mods, mods, g, w_bf)


def _depthwise_taps(win, taps, width):
    nwin = CONV_ROWS + 2 * CONV_PAD
    acc = jnp.zeros((CONV_ROWS, win.shape[1]), F32)
    for res in range(8):
        offsets = [o for o in range(res, nwin - CONV_ROWS + 1, 8)
                   if 0 <= o - CONV_PAD + width // 2 < width]
        if not offsets:
            continue
        rolled = win if res == 0 else pltpu.roll(win, nwin - res, 0)
        for o in offsets:
            k = o - CONV_PAD + width // 2
            acc = acc + rolled[o - res:o - res + CONV_ROWS, :] * taps[k:k + 1, :]
    return acc


def _conv_kernel(bg_ref, cg_ref, xa_ref, ga_ref, gb_ref, ca_ref, dw_ref, dwb_ref,
                 ya_ref, zc_ref, vs_ref, zs_ref):
    ca = ca_ref[...]
    dw = dw_ref[...]
    dwb = dwb_ref[...]
    zeros_pad = jnp.zeros((CONV_PAD, CONV_CW), F32)

    def run_sequence(seq0, length):
        vs_ref[pl.ds(0, CONV_PAD), :] = zeros_pad
        zs_ref[pl.ds(0, CONV_PAD), :] = zeros_pad
        vs_ref[pl.ds(CONV_PAD + length, CONV_PAD), :] = zeros_pad
        zs_ref[pl.ds(CONV_PAD + length, CONV_PAD), :] = zeros_pad

        def fill(i, carry):
            r = pl.multiple_of(i * ROW_TILE, ROW_TILE)
            src = pl.ds(seq0 + r, ROW_TILE)
            dst = pl.ds(CONV_PAD + r, ROW_TILE)
            vs_ref[dst, :] = cg_ref[src, :].astype(F32) * xa_ref[src, :].astype(F32)
            zs_ref[dst, :] = ga_ref[src, :].astype(F32) * _sigmoid(gb_ref[src, :].astype(F32))
            return carry

        lax.fori_loop(0, length // ROW_TILE, fill, 0)

        def conv(i, carry):
            r = pl.multiple_of(i * CONV_ROWS, CONV_ROWS)
            win = pl.ds(r, CONV_ROWS + 2 * CONV_PAD)
            out = pl.ds(seq0 + r, CONV_ROWS)
            acc3 = _depthwise_taps(vs_ref[win, :], ca, SC_WIDTH)
            ya_ref[out, :] = (bg_ref[out, :].astype(F32) * acc3).astype(BF16)
            acc = _depthwise_taps(zs_ref[win, :], dw, CF_KERNEL)
            zc_ref[out, :] = (acc + dwb).astype(BF16)
            return carry

        lax.fori_loop(0, length // CONV_ROWS, conv, 0)

    run_sequence(0, CTX)
    run_sequence(CTX, SEQ)


def _conv(u, conv_a, dw_w, dw_b, j):
    nblk = SC_W // CONV_CW
    part = lambda k: pl.BlockSpec((None, T, CONV_CW), lambda b, cb: (b, 0, k * nblk + cb))
    out_spec = pl.BlockSpec((None, T, CONV_CW), lambda b, cb: (b, 0, cb))
    return pl.pallas_call(
        _conv_kernel,
        grid=(B, nblk),
        in_specs=[part(0), part(1), part(2), part(3), part(4),
                  pl.BlockSpec((None, SC_WIDTH, CONV_CW), lambda b, cb: (j, 0, cb)),
                  pl.BlockSpec((None, CF_KERNEL, CONV_CW), lambda b, cb: (j, 0, cb)),
                  pl.BlockSpec((None, 1, CONV_CW), lambda b, cb: (j, 0, cb))],
        out_specs=[out_spec, out_spec],
        out_shape=[jax.ShapeDtypeStruct((B, T, HALF), BF16)] * 2,
        scratch_shapes=[pltpu.VMEM((SEQ + 2 * CONV_PAD, CONV_CW), F32)] * 2,
        compiler_params=pltpu.CompilerParams(
            dimension_semantics=("arbitrary", "arbitrary"), vmem_limit_bytes=VMEM_LIMIT),
        name="conv_mixers",
    )(u, u, u, u, u, conv_a, dw_w, dw_b.reshape(dw_b.shape[0], 1, dw_b.shape[1]))


N_RPB_DR = 2 * NA_ROWS - 1
N_RPB_DC = 2 * NA_COLS - 1
ATT_MASKED = N_RPB_DR


def _att_block_geometry(blk):
    r0 = blk * ATT_QROWS
    return min(max(r0 - NA_ROWS // 2, 0), GRID_ROWS - ATT_KROWS)


def _softmax_pv(parts):
    mx = None
    for s, _ in parts:
        pm = jnp.max(s, axis=-1, keepdims=True)
        mx = pm if mx is None else jnp.maximum(mx, pm)
    den = None
    num = None
    for s, v in parts:
        e = jnp.exp(s - mx)
        ps = jnp.sum(e, axis=-1, keepdims=True)
        pv = jnp.dot(e.astype(BF16), v, preferred_element_type=F32)
        den = ps if den is None else den + ps
        num = pv if num is None else num + pv
    return num / den


def _att_kernel(rpb_ref, q_ref, k_ref, v_ref, o_ref, tile_ref, *, rpb_base):
    @pl.when(pl.program_id(1) == 0)
    def _():
        base = (rpb_base + pl.program_id(0)) * (N_RPB_DR * N_RPB_DC)
        ci = lax.broadcasted_iota(I32, (GRID_W, 2 * GRID_W), 0)
        wi = lax.broadcasted_iota(I32, (GRID_W, 2 * GRID_W), 1) & (GRID_W - 1)
        cs = jnp.clip(ci - NA_COLS // 2, 0, GRID_W - NA_COLS)
        dsel = jnp.where((wi >= cs) & (wi < cs + NA_COLS), wi - ci + (NA_COLS - 1), -1)
        masked = jnp.full((GRID_W, 2 * GRID_W), NEG_INF, F32)
        for dr in range(N_RPB_DR):
            t = masked
            for dc in range(N_RPB_DC):
                t = jnp.where(dsel == dc, rpb_ref[base + dr * N_RPB_DC + dc], t)
            tile_ref[dr] = t
        tile_ref[ATT_MASKED] = masked

    def block_bias(blk, kr0):
        low_half = lax.broadcasted_iota(I32, (GRID_W, 2 * GRID_W), 1) < GRID_W
        bands = []
        for qr in range(ATT_QROWS):
            r = blk * ATT_QROWS + qr
            rs = min(max(r - NA_ROWS // 2, 0), GRID_ROWS - NA_ROWS)
            tiles = []
            for jp in range(ATT_KROWS // 2):
                idx = [kr - r + (NA_ROWS - 1) if rs <= kr < rs + NA_ROWS else ATT_MASKED
                       for kr in (kr0 + 2 * jp, kr0 + 2 * jp + 1)]
                if idx[0] == idx[1]:
                    tiles.append(tile_ref[idx[0]])
                else:
                    tiles.append(jnp.where(low_half, tile_ref[idx[0]], tile_ref[idx[1]]))
            bands.append(jnp.concatenate(tiles, axis=1))
        return jnp.concatenate(bands, axis=0)

    scale = NA_DH ** -0.5
    nt = (((1,), (1,)), ((), ()))
    kc = k_ref[pl.ds(0, CTX), :]
    vc = v_ref[pl.ds(0, CTX), :]

    qc = q_ref[pl.ds(0, CTX), :]
    s_cc = lax.dot_general(qc, kc, nt, preferred_element_type=F32) * scale
    o_ref[pl.ds(0, CTX), :] = _softmax_pv([(s_cc, vc)]).astype(BF16)

    for blk in range(GRID_ROWS // ATT_QROWS):
        kr0 = _att_block_geometry(blk)
        qrows = pl.ds(CTX + blk * ATT_Q, ATT_Q)
        krows = pl.ds(CTX + kr0 * GRID_W, ATT_K)
        qb = q_ref[qrows, :]
        s_w = lax.dot_general(qb, k_ref[krows, :], nt, preferred_element_type=F32) * scale
        s_w = s_w + block_bias(blk, kr0)
        s_c = lax.dot_general(qb, kc, nt, preferred_element_type=F32) * scale
        o_ref[qrows, :] = _softmax_pv([(s_w, v_ref[krows, :]), (s_c, vc)]).astype(BF16)


def _attention(u, rpb_flat, j):
    col = lambda off: pl.BlockSpec((None, T, NA_DH), lambda h, b: (b, 0, off + h))
    return pl.pallas_call(
        functools.partial(_att_kernel, rpb_base=j * NA_HEADS),
        grid=(NA_HEADS, B),
        in_specs=[pl.BlockSpec(memory_space=pltpu.SMEM),
                  col(0), col(NA_HEADS), col(2 * NA_HEADS)],
        out_specs=pl.BlockSpec((None, T, NA_DH), lambda h, b: (b, 0, h)),
        out_shape=jax.ShapeDtypeStruct((B, T, HALF), BF16),
        scratch_shapes=[pltpu.VMEM((N_RPB_DR + 1, GRID_W, 2 * GRID_W), F32)],
        compiler_params=pltpu.CompilerParams(
            dimension_semantics=("arbitrary", "arbitrary"), vmem_limit_bytes=VMEM_LIMIT),
        name="neighbourhood_attention",
    )(rpb_flat, u, u, u)


POOL_KWIN = 512


def _pool_kernel(p_ref, pw_ref, ps_ref, o_ref):
    half = lax.shift_left(jnp.int32(1), pl.program_id(1))
    pw = pw_ref[...].astype(BF16)
    sc = ps_ref[...]

    def tile(row0, seq0, length, ks, kwin):
        t = (row0 - seq0) + lax.broadcasted_iota(I32, (ROW_TILE, 1), 0)
        s = ks + lax.broadcasted_iota(I32, (1, kwin), 1)
        lo = t - half
        hi = t + half - 1
        band = jnp.where((s >= lo) & (s <= hi), 1.0, 0.0).astype(BF16)
        ssum = jnp.dot(band, p_ref[pl.ds(seq0 + ks, kwin), :], preferred_element_type=F32)
        cnt = (jnp.minimum(hi, length - 1) - jnp.maximum(lo, 0) + 1).astype(F32)
        mix = ssum / cnt - p_ref[pl.ds(row0, ROW_TILE), :].astype(F32)
        y = jnp.dot(mix.astype(BF16), pw, preferred_element_type=F32) * sc
        o_ref[pl.ds(row0, ROW_TILE), :] = y.astype(BF16)

    tile(0, 0, CTX, 0, CTX)
    for jt in range(SEQ // ROW_TILE):
        ks = min(max(ROW_TILE * jt - (POOL_KWIN - ROW_TILE) // 2, 0), SEQ - POOL_KWIN)
        tile(CTX + ROW_TILE * jt, CTX, SEQ, ks, POOL_KWIN)


def _pool(u, pool_w, pool_scale, j):
    p_block0 = 3 * HALF // POOL_GC
    return pl.pallas_call(
        _pool_kernel,
        grid=(B, N_POOL),
        in_specs=[pl.BlockSpec((None, T, POOL_GC), lambda b, g: (b, 0, p_block0 + g)),
                  pl.BlockSpec((None, None, POOL_GC, POOL_GC), lambda b, g: (j, g, 0, 0)),
                  pl.BlockSpec((None, None, 1, POOL_GC), lambda b, g: (j, g, 0, 0))],
        out_specs=pl.BlockSpec((None, T, POOL_GC), lambda b, g: (b, 0, g)),
        out_shape=jax.ShapeDtypeStruct((B, T, HALF), BF16),
        compiler_params=pltpu.CompilerParams(
            dimension_semantics=("arbitrary", "arbitrary"), vmem_limit_bytes=VMEM_LIMIT),
        name="pool_mixer",
    )(u, pool_w, pool_scale.reshape(pool_scale.shape[0], N_POOL, 1, POOL_GC))


ROUTE_LANES = 128


def _route(logits):
    lane = lax.broadcasted_iota(I32, logits.shape, 1)
    lanef = lane.astype(F32)
    is_g = lane < N_GROUPS
    gl = jnp.where(is_g, logits, NEG_INF)
    ge = jnp.where(is_g, jnp.exp(gl - jnp.max(gl, axis=-1, keepdims=True)), 0.0)
    gp = ge / jnp.sum(ge, axis=-1, keepdims=True)
    g_p = jnp.max(gp, axis=-1, keepdims=True)
    g_idx = jnp.min(jnp.where(is_g & (gp == g_p), lanef, float(ROUTE_LANES)), axis=-1, keepdims=True)
    first = float(N_GROUPS) + float(EPG) * g_idx
    sel = (lanef >= first) & (lanef < first + float(EPG))
    el = jnp.where(sel, logits, NEG_INF)
    ee = jnp.where(sel, jnp.exp(el - jnp.max(el, axis=-1, keepdims=True)), 0.0)
    ep = jnp.where(sel, ee / jnp.sum(ee, axis=-1, keepdims=True), -1.0)
    p1 = jnp.max(ep, axis=-1, keepdims=True)
    i1 = jnp.min(jnp.where(ep == p1, lanef, float(ROUTE_LANES)), axis=-1, keepdims=True)
    ep2 = jnp.where(lanef == i1, -1.0, ep)
    p2 = jnp.max(ep2, axis=-1, keepdims=True)
    i2 = jnp.min(jnp.where(ep2 == p2, lanef, float(ROUTE_LANES)), axis=-1, keepdims=True)
    psum = p1 + p2
    gate1 = g_p * p1 / psum
    gate2 = g_p * p2 / psum
    return jnp.where(lane == 0, i1 - float(N_GROUPS),
                     jnp.where(lane == 1, i2 - float(N_GROUPS),
                               jnp.where(lane == 2, gate1,
                                         jnp.where(lane == 3, gate2, 0.0))))


N_ROW_TILES = NTOK // ROW_TILE
TILES_PER_BATCH = T // ROW_TILE


def _proj_kernel(ya_ref, yb_ref, w_ref, lng_ref, lnb_ref, x_ref, modb_ref, modc_ref, g2_ref,
                 rwh_ref, rwl_ref, rb_ref, xo_ref, hp_ref, ri_ref,
                 lhs_s, out_a, out_b, hh_s, hl_s, vec_s, *, even):
    s = pl.program_id(0)
    pieces = ROW_TILE // NM_CHUNK

    @pl.when(s == 0)
    def _():
        out_b[...] = jnp.zeros_like(out_b)

    is_ctx = jnp.maximum(s - 1, 0) % TILES_PER_BATCH == 0
    mod = jnp.where(is_ctx, modc_ref[...], modb_ref[...])
    vec_s[0:1, :] = mod[2:3, :]
    vec_s[1:2, :] = g2_ref[...] * (1.0 + mod[4:5, :])
    vec_s[2:3, :] = mod[3:4, :]

    def step(mm_out, epi_in):
        for c in range(pieces):
            rows = pl.ds(c * NM_CHUNK, NM_CHUNK)
            xn = x_ref[rows, :] + vec_s[0:1, :] * epi_in[rows, :]
            xo_ref[rows, :] = xn
            inv = lax.rsqrt(jnp.mean(xn * xn, axis=-1, keepdims=True) + RMS_EPS)
            h2 = xn * inv * vec_s[1:2, :] + vec_s[2:3, :]
            hp_ref[rows, :] = h2
            hb = h2.astype(BF16)
            hh_s[rows, :] = hb
            hl_s[rows, :] = (h2 - hb.astype(F32)).astype(BF16)
        hh = hh_s[...]
        logits = (jnp.dot(hh, rwh_ref[...], preferred_element_type=F32)
                  + jnp.dot(hl_s[...], rwh_ref[...], preferred_element_type=F32)
                  + jnp.dot(hh, rwl_ref[...], preferred_element_type=F32)) + rb_ref[...]
        ri_ref[...] = _route(logits)

        if even:
            for c in range(pieces):
                rows = pl.ds(c * NM_CHUNK, NM_CHUNK)
                zc = yb_ref[rows, :].astype(F32)
                dz = zc - jnp.mean(zc, axis=-1, keepdims=True)
                var = jnp.mean(dz * dz, axis=-1, keepdims=True)
                z = dz * lax.rsqrt(var + LN_EPS) * lng_ref[...] + lnb_ref[...]
                lhs_s[rows, pl.ds(HALF, HALF)] = (z * _sigmoid(z)).astype(BF16)
        else:
            lhs_s[:, pl.ds(HALF, HALF)] = yb_ref[...]
        lhs_s[:, pl.ds(0, HALF)] = ya_ref[...]
        for n in range(D // NM_TN):
            cols = pl.ds(n * NM_TN, NM_TN)
            mm_out[:, cols] = jnp.dot(lhs_s[...], w_ref[:, cols], preferred_element_type=F32)

    @pl.when(s % 2 == 0)
    def _():
        step(out_a, out_b)

    @pl.when(s % 2 == 1)
    def _():
        step(out_b, out_a)


def _proj(ya, yb, w_bf, widx, ln_g, ln_b, xc, mods, layer, g2, rw_hi, rw_lo, rb, even):
    def tile_of(t):
        return t // TILES_PER_BATCH, t % TILES_PER_BATCH

    def ahead(width):
        return pl.BlockSpec((None, ROW_TILE, width),
                            lambda s: (*tile_of(jnp.minimum(s, N_ROW_TILES - 1)), 0))

    def behind(width):
        return pl.BlockSpec((None, ROW_TILE, width), lambda s: (*tile_of(jnp.maximum(s - 1, 0)), 0))

    const = lambda *shape: pl.BlockSpec((None,) + shape, lambda s: (widx,) + (0,) * len(shape))
    per_layer = lambda *shape: pl.BlockSpec((None,) + shape, lambda s: (layer,) + (0,) * len(shape))
    return pl.pallas_call(
        functools.partial(_proj_kernel, even=even),
        grid=(N_ROW_TILES + 1,),
        in_specs=[ahead(HALF), ahead(HALF),
                  const(D, D), const(1, HALF), const(1, HALF),
                  behind(D),
                  pl.BlockSpec((None, None, 6, D),
                               lambda s: (layer, jnp.maximum(s - 1, 0) // TILES_PER_BATCH, 0, 0)),
                  pl.BlockSpec((None, None, 6, D), lambda s: (layer, B, 0, 0)),
                  per_layer(1, D), per_layer(D, ROUTE_LANES), per_layer(D, ROUTE_LANES),
                  per_layer(1, ROUTE_LANES)],
        out_specs=[behind(D), behind(D), behind(ROUTE_LANES)],
        out_shape=[jax.ShapeDtypeStruct((B, T, D), F32),
                   jax.ShapeDtypeStruct((B, T, D), F32),
                   jax.ShapeDtypeStruct((B, T, ROUTE_LANES), F32)],
        scratch_shapes=[pltpu.VMEM((ROW_TILE, D), BF16),
                        pltpu.VMEM((ROW_TILE, D), F32),
                        pltpu.VMEM((ROW_TILE, D), F32),
                        pltpu.VMEM((ROW_TILE, D), BF16),
                        pltpu.VMEM((ROW_TILE, D), BF16),
                        pltpu.VMEM((8, D), F32)],
        compiler_params=pltpu.CompilerParams(
            dimension_semantics=("arbitrary",), vmem_limit_bytes=VMEM_LIMIT),
        name="out_proj_router",
    )(ya, yb, w_bf, ln_g, ln_b, xc, mods, mods, g2, rw_hi, rw_lo, rb)


def _positions_kernel(ri_ref, pos_ref, base_ref):
    ntiles = NTOK // ROW_TILE
    lanef = lax.broadcasted_iota(I32, (ROW_TILE, ROUTE_LANES), 1).astype(F32)

    def count(i, acc):
        r = ri_ref[pl.ds(pl.multiple_of(i * ROW_TILE, ROW_TILE), ROW_TILE), :]
        hit = (lanef == r[:, 0:1]) | (lanef == r[:, 1:2])
        return acc + jnp.sum(jnp.where(hit, 1.0, 0.0), axis=0, keepdims=True)

    cnt = lax.fori_loop(0, ntiles, count, jnp.zeros((1, ROUTE_LANES), F32))
    ri = lax.broadcasted_iota(I32, (ROUTE_LANES, ROUTE_LANES), 0)
    ci = lax.broadcasted_iota(I32, (ROUTE_LANES, ROUTE_LANES), 1)
    before = jnp.where(ri < ci, 1.0, 0.0)
    cnt8 = jnp.broadcast_to(cnt, (8, ROUTE_LANES))
    base8 = jnp.dot(cnt8, before, preferred_element_type=F32, precision=HIGHEST)
    base_ref[...] = base8.astype(I32)
    base_col = jnp.broadcast_to(base8[0:1, :], (ROUTE_LANES, ROUTE_LANES)).T[:, 0:1]

    ti = lax.broadcasted_iota(I32, (ROW_TILE, ROW_TILE), 0)
    tj = lax.broadcasted_iota(I32, (ROW_TILE, ROW_TILE), 1)
    earlier = jnp.where(ti < tj, 1.0, 0.0).astype(BF16)
    subf = lax.broadcasted_iota(I32, (ROUTE_LANES, ROW_TILE), 0).astype(F32)

    def place(i, carry):
        start = pl.multiple_of(i * ROW_TILE, ROW_TILE)
        rt = ri_ref[pl.ds(start, ROW_TILE), :].T
        e0 = rt[0:1, :]
        e1 = rt[1:2, :]
        hit0 = subf == e0
        hit1 = subf == e1
        hits = jnp.where(hit0 | hit1, 1.0, 0.0)
        rank = jnp.dot(hits.astype(BF16), earlier, preferred_element_type=F32) + carry + base_col
        pos_ref[0:1, pl.ds(start, ROW_TILE)] = jnp.sum(
            jnp.where(hit0, rank, 0.0), axis=0, keepdims=True).astype(I32)
        pos_ref[1:2, pl.ds(start, ROW_TILE)] = jnp.sum(
            jnp.where(hit1, rank, 0.0), axis=0, keepdims=True).astype(I32)
        return carry + jnp.sum(hits, axis=1, keepdims=True)

    lax.fori_loop(0, ntiles, place, jnp.zeros((ROUTE_LANES, 1), F32))


def _positions(ri_flat):
    return pl.pallas_call(
        _positions_kernel,
        out_shape=[jax.ShapeDtypeStruct((2, NTOK), I32),
                   jax.ShapeDtypeStruct((8, ROUTE_LANES), I32)],
        compiler_params=pltpu.CompilerParams(vmem_limit_bytes=VMEM_LIMIT),
        name="moe_positions",
    )(ri_flat)


INVERT_UNROLL = 8


def _items_kernel(base_ref, pos_ref, it_ref, src_ref):
    def init(i, carry):
        it_ref[ITEM_NEW + i] = 0
        it_ref[ITEM_SLOT + i] = 0
        it_ref[ITEM_NEXT + i] = -1
        return carry

    lax.fori_loop(0, N_ITEMS, init, 0)

    def per_expert(e, carry):
        n, run, run_start = carry
        s = base_ref[e]
        t = base_ref[e + 1]
        first = lax.shift_right_logical(s, 8)
        last = lax.shift_right_logical(jnp.maximum(t - 1, 0), 8)
        used = t > s
        ntile = jnp.where(used, last - first + 1, 0)

        def per_tile(k, n2):
            tile = first + k
            it_ref[n2] = tile
            it_ref[N_ITEMS + n2] = e
            it_ref[2 * N_ITEMS + n2] = jnp.maximum(s - tile * ROW_TILE, 0)
            it_ref[3 * N_ITEMS + n2] = jnp.minimum(t - tile * ROW_TILE, ROW_TILE)
            return n2 + 1

        n_end = lax.fori_loop(0, ntile, per_tile, n)
        it_ref[ITEM_NEW + n] = jnp.where(used, 1, it_ref[ITEM_NEW + n])
        it_ref[ITEM_SLOT + n] = jnp.where(used, run & 1, it_ref[ITEM_SLOT + n])
        link = used & (run > 0)
        it_ref[ITEM_NEXT + run_start] = jnp.where(link, e, it_ref[ITEM_NEXT + run_start])
        return n_end, run + jnp.where(used, 1, 0), jnp.where(used, n, run_start)

    n_used, _, _ = lax.fori_loop(0, N_EXPERTS, per_expert,
                                 (jnp.int32(0), jnp.int32(0), jnp.int32(0)))
    last_e = it_ref[N_ITEMS + n_used - 1]

    def pad(i, carry):
        it_ref[i] = N_SORT_TILES - 1
        it_ref[N_ITEMS + i] = last_e
        it_ref[2 * N_ITEMS + i] = 0
        it_ref[3 * N_ITEMS + i] = 0
        return carry

    lax.fori_loop(n_used, N_ITEMS, pad, 0)

    def invert(g, carry):
        for k in range(INVERT_UNROLL):
            t = g * INVERT_UNROLL + k
            src_ref[pos_ref[t]] = t
            src_ref[pos_ref[NTOK + t]] = t
        return carry

    lax.fori_loop(0, NTOK // INVERT_UNROLL, invert, 0)


def _items(base_flat, pos_flat):
    smem = pl.BlockSpec(memory_space=pltpu.SMEM)
    return pl.pallas_call(
        _items_kernel,
        in_specs=[smem, smem],
        out_specs=[smem, smem],
        out_shape=[jax.ShapeDtypeStruct((7 * N_ITEMS,), I32),
                   jax.ShapeDtypeStruct((NSLOT,), I32)],
        name="moe_items",
    )(base_flat, pos_flat)


ROW_DMA_UNROLL = 8


def _gmm_kernel(it_ref, src_ref, h_hbm, w1_hbm, w3_hbm, w2_hbm, o_ref,
                xbuf, w1_raw, w3_raw, w2_raw, w1_bf, w3_bf, w2_bf, sems, row_sems, *, layer):
    i = pl.program_id(0)
    tile = it_ref[i]
    par = tile & 1

    def gather_rows(tl, p):
        def rows(g, carry):
            for k in range(ROW_DMA_UNROLL):
                r = g * ROW_DMA_UNROLL + k
                pltpu.make_async_copy(h_hbm.at[pl.ds(src_ref[tl * ROW_TILE + r], 1)],
                                      xbuf.at[p, pl.ds(r, 1)], row_sems.at[p]).start()
            return carry

        lax.fori_loop(0, ROW_TILE // ROW_DMA_UNROLL, rows, 0)
    expert = it_ref[N_ITEMS + i]
    lo = it_ref[2 * N_ITEMS + i]
    hi = it_ref[3 * N_ITEMS + i]
    slot = it_ref[ITEM_SLOT + i]
    nxt = it_ref[ITEM_NEXT + i]
    prev_tile = it_ref[jnp.maximum(i - 1, 0)]

    def weight_copies(e, s):
        return (pltpu.make_async_copy(w1_hbm.at[layer, e], w1_raw.at[s], sems.at[s, 0]),
                pltpu.make_async_copy(w3_hbm.at[layer, e], w3_raw.at[s], sems.at[s, 1]),
                pltpu.make_async_copy(w2_hbm.at[layer, e], w2_raw.at[s], sems.at[s, 2]))

    @pl.when(it_ref[ITEM_NEW + i] == 1)
    def _():
        @pl.when(i == 0)
        def _():
            for cp in weight_copies(expert, slot):
                cp.start()

        for cp in weight_copies(expert, slot):
            cp.wait()

        @pl.when(nxt >= 0)
        def _():
            for cp in weight_copies(nxt, 1 - slot):
                cp.start()

        w1_bf[...] = w1_raw[slot].astype(BF16)
        w3_bf[...] = w3_raw[slot].astype(BF16)
        w2_bf[...] = w2_raw[slot].astype(BF16)

    @pl.when((i == 0) | (tile != prev_tile))
    def _():
        @pl.when(i == 0)
        def _():
            gather_rows(tile, par)

        pltpu.make_async_copy(xbuf.at[par], xbuf.at[par], row_sems.at[par]).wait()

        @pl.when(tile + 1 < N_SORT_TILES)
        def _():
            gather_rows(tile + 1, 1 - par)

        o_ref[...] = jnp.zeros_like(o_ref)

    @pl.when(hi > lo)
    def _():
        xb = xbuf[par].astype(BF16)
        a = jnp.dot(xb, w1_bf[...], preferred_element_type=F32)
        b = jnp.dot(xb, w3_bf[...], preferred_element_type=F32)
        rows = lax.broadcasted_iota(I32, (ROW_TILE, 1), 0)
        hid = jnp.where((rows >= lo) & (rows < hi), a * _sigmoid(a) * b, 0.0)
        o_ref[...] += jnp.dot(hid.astype(BF16), w2_bf[...], preferred_element_type=F32)


def _gmm(items, src, h_flat, w1, w3, w2, layer):
    hbm = pl.BlockSpec(memory_space=pl.ANY)
    return pl.pallas_call(
        functools.partial(_gmm_kernel, layer=layer),
        grid_spec=pltpu.PrefetchScalarGridSpec(
            num_scalar_prefetch=2,
            grid=(N_ITEMS,),
            in_specs=[hbm, hbm, hbm, hbm],
            out_specs=pl.BlockSpec((ROW_TILE, D), lambda i, it, src: (it[i], 0)),
            scratch_shapes=[pltpu.VMEM((2, ROW_TILE, D), F32),
                            pltpu.VMEM((2, D, EXPERT_FF), F32),
                            pltpu.VMEM((2, D, EXPERT_FF), F32),
                            pltpu.VMEM((2, EXPERT_FF, D), F32),
                            pltpu.VMEM((D, EXPERT_FF), BF16),
                            pltpu.VMEM((D, EXPERT_FF), BF16),
                            pltpu.VMEM((EXPERT_FF, D), BF16),
                            pltpu.SemaphoreType.DMA((2, 3)),
                            pltpu.SemaphoreType.DMA((2,))],
        ),
        out_shape=jax.ShapeDtypeStruct((NSLOT, D), F32),
        compiler_params=pltpu.CompilerParams(
            dimension_semantics=("arbitrary",), vmem_limit_bytes=VMEM_LIMIT),
        name="moe_experts",
    )(items, src, h_flat, w1, w3, w2)


def _combine_kernel(pos_ref, ys_hbm, x_ref, ri_ref, modb_ref, modc_ref, fg_ref, o_ref, buf, gate_s, sems,
                    *, final, tiles_per_b, m_off):
    i = pl.program_id(0)
    ntiles = pl.num_programs(0)

    def issue(tile_i, par):
        t0 = (tile_i // tiles_per_b) * T + (tile_i % tiles_per_b + m_off) * ROW_TILE

        def rows(g, carry):
            for k in range(ROW_DMA_UNROLL):
                r = g * ROW_DMA_UNROLL + k
                for s in range(2):
                    pltpu.make_async_copy(ys_hbm.at[pl.ds(pos_ref[s * NTOK + t0 + r], 1)],
                                          buf.at[par, s, pl.ds(r, 1)], sems.at[par]).start()
            return carry

        lax.fori_loop(0, ROW_TILE // ROW_DMA_UNROLL, rows, 0)

    @pl.when(i == 0)
    def _():
        issue(0, 0)

    @pl.when(i + 1 < ntiles)
    def _():
        issue(i + 1, (i + 1) % 2)

    par = i % 2
    pltpu.make_async_copy(buf.at[par], buf.at[par], sems.at[par]).wait()
    m = i % tiles_per_b + m_off
    gate_s[...] = jnp.where(m == 0, modc_ref[5:6, :], modb_ref[5:6, :])
    for c in range(ROW_TILE // NM_CHUNK):
        rows = pl.ds(c * NM_CHUNK, NM_CHUNK)
        g0 = ri_ref[rows, 2:3]
        g1 = ri_ref[rows, 3:4]
        xn = x_ref[rows, :] + gate_s[...] * (g0 * buf[par, 0, rows, :] + g1 * buf[par, 1, rows, :])
        if final:
            inv = lax.rsqrt(jnp.mean(xn * xn, axis=-1, keepdims=True) + RMS_EPS)
            xn = xn * inv * fg_ref[...]
        o_ref[rows, :] = xn


def _combine(pos_flat, ys, xc, ri, mods, layer, final_g, final):
    if final:
        tiles_per_b, m_off, rows_out = SEQ // ROW_TILE, CTX // ROW_TILE, SEQ
    else:
        tiles_per_b, m_off, rows_out = T // ROW_TILE, 0, T
    tok = lambda width: pl.BlockSpec(
        (None, ROW_TILE, width), lambda i, pos: (i // tiles_per_b, i % tiles_per_b + m_off, 0))
    return pl.pallas_call(
        functools.partial(_combine_kernel, final=final, tiles_per_b=tiles_per_b, m_off=m_off),
        grid_spec=pltpu.PrefetchScalarGridSpec(
            num_scalar_prefetch=1,
            grid=(B * tiles_per_b,),
            in_specs=[
                pl.BlockSpec(memory_space=pl.ANY),
                tok(D),
                tok(ROUTE_LANES),
                pl.BlockSpec((None, None, 6, D), lambda i, pos: (layer, i // tiles_per_b, 0, 0)),
                pl.BlockSpec((None, None, 6, D), lambda i, pos: (layer, B, 0, 0)),
                pl.BlockSpec((1, D), lambda i, pos: (0, 0)),
            ],
            out_specs=pl.BlockSpec((None, ROW_TILE, D),
                                   lambda i, pos: (i // tiles_per_b, i % tiles_per_b, 0)),
            scratch_shapes=[pltpu.VMEM((2, 2, ROW_TILE, D), F32),
                            pltpu.VMEM((1, D), F32),
                            pltpu.SemaphoreType.DMA((2,))],
        ),
        out_shape=jax.ShapeDtypeStruct((B, rows_out, D), F32),
        compiler_params=pltpu.CompilerParams(
            dimension_semantics=("arbitrary",), vmem_limit_bytes=VMEM_LIMIT),
        name="moe_combine",
    )(pos_flat, ys, xc, ri, mods, mods, final_g)


def kernel(x, c, ctx, c_ctx, ada_w, ada_b, norm1_g, norm2_g, ev_w_in, ev_conv_a, ev_dw_w, ev_dw_b,
           ev_ln_g, ev_ln_b, ev_w_out, od_w_in, od_rpb, od_pool_w, od_pool_scale, od_w_out,
           moe_rg_w, moe_rg_b, moe_re_w, moe_re_b, moe_w1, moe_w3, moe_w2, final_g):
    xc = jnp.concatenate([ctx, x], axis=1)
    c8 = jnp.concatenate([c, c_ctx[None, :], jnp.zeros((8 - B - 1, D), F32)], axis=0)
    mods = _ada(c8, ada_w, ada_b).reshape(DEPTH, 8, 6, D)

    ev_w_in_bf = ev_w_in.astype(BF16)
    od_w_in_bf = od_w_in.astype(BF16)
    ev_w_out_bf = ev_w_out.astype(BF16)
    od_w_out_bf = od_w_out.astype(BF16)
    pad = ROUTE_LANES - N_GROUPS - N_EXPERTS
    rw = jnp.concatenate([moe_rg_w, moe_re_w, jnp.zeros((DEPTH, D, pad), F32)], axis=-1)
    rw_hi = rw.astype(BF16)
    rw_lo = (rw - rw_hi.astype(F32)).astype(BF16)
    rb = jnp.concatenate([moe_rg_b, moe_re_b, jnp.zeros((DEPTH, pad), F32)],
                         axis=-1).reshape(DEPTH, 1, ROUTE_LANES)
    final_g2 = final_g.reshape(1, D)
    norm1_g = norm1_g.reshape(DEPTH, 1, D)
    norm2_g = norm2_g.reshape(DEPTH, 1, D)
    ev_ln_g = ev_ln_g.reshape(-1, 1, HALF)
    ev_ln_b = ev_ln_b.reshape(-1, 1, HALF)

    for i in range(DEPTH):
        j = i // 2
        last = i == DEPTH - 1
        if i % 2 == 1:
            u = _norm_mm(xc, mods, i, norm1_g, od_w_in_bf, j)
            ya = _attention(u, od_rpb.reshape(-1), j)
            yb = _pool(u, od_pool_w, od_pool_scale, j)
            xc, hp, ri = _proj(ya, yb, od_w_out_bf, j, ev_ln_g, ev_ln_b, xc, mods, i, norm2_g,
                               rw_hi, rw_lo, rb, even=False)
        else:
            u = _norm_mm(xc, mods, i, norm1_g, ev_w_in_bf, j)
            ya, yb = _conv(u, ev_conv_a, ev_dw_w, ev_dw_b, j)
            xc, hp, ri = _proj(ya, yb, ev_w_out_bf, j, ev_ln_g, ev_ln_b, xc, mods, i, norm2_g,
                               rw_hi, rw_lo, rb, even=True)
        pos, base = _positions(ri.reshape(NTOK, ROUTE_LANES))
        pos_flat = pos.reshape(2 * NTOK)
        items, src = _items(base[0, :N_EXPERTS + 1 + 7], pos_flat)
        ys = _gmm(items, src, hp.reshape(NTOK, D), moe_w1, moe_w3, moe_w2, i)
        xc = _combine(pos_flat, ys, xc, ri, mods, i, final_g2, last)
    return xc
```

```python
import functools

import numpy as np
import jax
import jax.numpy as jnp
from jax import lax
from jax.experimental import pallas as pl
from jax.experimental.pallas import tpu as pltpu

F32 = jnp.float32
BF16 = jnp.bfloat16
I32 = jnp.int32
HIGHEST = lax.Precision.HIGHEST

D = 2048
B = 4
SEQ = 2048
CTX = 256
T = CTX + SEQ
NTOK = B * T
DEPTH = 4
GRID_W = 64
GRID_ROWS = SEQ // GRID_W
HALF = D // 2
SC_W = 1024
CF_KERNEL = 31
SC_WIDTH = 3
NA_HEADS = 8
NA_DH = 128
NA_ROWS = 8
NA_COLS = 16
N_POOL = 4
POOL_GC = 256
N_EXPERTS = 32
N_GROUPS = 4
EPG = 8
EXPERT_FF = 256
RMS_EPS = 1e-6
LN_EPS = 1e-5
NEG_INF = -1e30

ROW_TILE = 256
NSLOT = 2 * NTOK
N_SORT_TILES = NSLOT // ROW_TILE
N_ITEMS = 104
ITEM_NEW, ITEM_SLOT, ITEM_NEXT = 4 * N_ITEMS, 5 * N_ITEMS, 6 * N_ITEMS
NM_TN = 1024
NM_CHUNK = 16
ATT_QROWS = 4
ATT_Q = ATT_QROWS * GRID_W
ATT_KROWS = 12
ATT_K = ATT_KROWS * GRID_W
CONV_CW = 256
CONV_PAD = 16
CONV_ROWS = 32
VMEM_LIMIT = 56 * 1024 * 1024


def _sigmoid(v):
    return 1.0 / (1.0 + jnp.exp(-v))


ADA_TN = 1024


def _ada_kernel(c_ref, w_ref, b_ref, o_ref):
    cv = c_ref[...]
    s = cv * _sigmoid(cv)
    s_hi = s.astype(BF16)
    s_lo = (s - s_hi.astype(F32)).astype(BF16)
    w = w_ref[0]
    w_hi = w.astype(BF16)
    w_lo = (w - w_hi.astype(F32)).astype(BF16)
    o_ref[0] = (jnp.dot(s_hi, w_hi, preferred_element_type=F32)
                + jnp.dot(s_lo, w_hi, preferred_element_type=F32)
                + jnp.dot(s_hi, w_lo, preferred_element_type=F32)) + b_ref[0]


def _ada(c8, ada_w, ada_b):
    n = ada_w.shape[-1]
    return pl.pallas_call(
        _ada_kernel,
        grid=(DEPTH, n // ADA_TN),
        in_specs=[
            pl.BlockSpec((8, D), lambda l, j: (0, 0)),
            pl.BlockSpec((1, D, ADA_TN), lambda l, j: (l, 0, j)),
            pl.BlockSpec((1, 1, ADA_TN), lambda l, j: (l, 0, j)),
        ],
        out_specs=pl.BlockSpec((1, 8, ADA_TN), lambda l, j: (l, 0, j)),
        out_shape=jax.ShapeDtypeStruct((DEPTH, 8, n), F32),
        compiler_params=pltpu.CompilerParams(
            dimension_semantics=("arbitrary", "arbitrary"), vmem_limit_bytes=VMEM_LIMIT),
        name="ada",
    )(c8, ada_w, ada_b.reshape(DEPTH, 1, n))


def _norm_mm_kernel(x_ref, modb_ref, modc_ref, g_ref, w_ref, o_ref, h_ref, gain_ref, shift_ref):
    is_ctx = pl.program_id(1) == 0
    gain_ref[...] = g_ref[...] * (1.0 + jnp.where(is_ctx, modc_ref[1:2, :], modb_ref[1:2, :]))
    shift_ref[...] = jnp.where(is_ctx, modc_ref[0:1, :], modb_ref[0:1, :])
    for c in range(ROW_TILE // NM_CHUNK):
        rows = pl.ds(c * NM_CHUNK, NM_CHUNK)
        xv = x_ref[rows, :]
        inv = lax.rsqrt(jnp.mean(xv * xv, axis=-1, keepdims=True) + RMS_EPS)
        h_ref[rows, :] = (xv * inv * gain_ref[...] + shift_ref[...]).astype(BF16)
    for n in range(o_ref.shape[1] // NM_TN):
        cols = pl.ds(n * NM_TN, NM_TN)
        o_ref[:, cols] = jnp.dot(h_ref[...], w_ref[:, cols],
                                 preferred_element_type=F32).astype(BF16)


def _norm_mm(xc, mods, layer, g, w_bf, widx):
    n_out = w_bf.shape[-1]
    return pl.pallas_call(
        _norm_mm_kernel,
        grid=(B, T // ROW_TILE),
        in_specs=[
            pl.BlockSpec((None, ROW_TILE, D), lambda b, m: (b, m, 0)),
            pl.BlockSpec((None, None, 6, D), lambda b, m: (layer, b, 0, 0)),
            pl.BlockSpec((None, None, 6, D), lambda b, m: (layer, B, 0, 0)),
            pl.BlockSpec((None, 1, D), lambda b, m: (layer, 0, 0)),
            pl.BlockSpec((None, D, n_out), lambda b, m: (widx, 0, 0), pipeline_mode=pl.Buffered(1)),
        ],
        out_specs=pl.BlockSpec((None, ROW_TILE, n_out), lambda b, m: (b, m, 0)),
        out_shape=jax.ShapeDtypeStruct((B, T, n_out), BF16),
        scratch_shapes=[pltpu.VMEM((ROW_TILE, D), BF16),
                        pltpu.VMEM((1, D), F32), pltpu.VMEM((1, D), F32)],
        compiler_params=pltpu.CompilerParams(
            dimension_semantics=("arbitrary", "arbitrary"), vmem_limit_bytes=VMEM_LIMIT),
        name="norm_mm",
    )(xc, mods, mods, g, w_bf)


def _depthwise_taps(win, taps, width):
    nwin = CONV_ROWS + 2 * CONV_PAD
    acc = jnp.zeros((CONV_ROWS, win.shape[1]), F32)
    for res in range(8):
        offsets = [o for o in range(res, nwin - CONV_ROWS + 1, 8)
                   if 0 <= o - CONV_PAD + width // 2 < width]
        if not offsets:
            continue
        rolled = win if res == 0 else pltpu.roll(win, nwin - res, 0)
        for o in offsets:
            k = o - CONV_PAD + width // 2
            acc = acc + rolled[o - res:o - res + CONV_ROWS, :] * taps[k:k + 1, :]
    return acc


def _conv_kernel(bg_ref, cg_ref, xa_ref, ga_ref, gb_ref, ca_ref, dw_ref, dwb_ref,
                 ya_ref, zc_ref, vs_ref, zs_ref):
    ca = ca_ref[...]
    dw = dw_ref[...]
    dwb = dwb_ref[...]
    zeros_pad = jnp.zeros((CONV_PAD, CONV_CW), F32)

    def run_sequence(seq0, length):
        vs_ref[pl.ds(0, CONV_PAD), :] = zeros_pad
        zs_ref[pl.ds(0, CONV_PAD), :] = zeros_pad
        vs_ref[pl.ds(CONV_PAD + length, CONV_PAD), :] = zeros_pad
        zs_ref[pl.ds(CONV_PAD + length, CONV_PAD), :] = zeros_pad

        def fill(i, carry):
            r = pl.multiple_of(i * ROW_TILE, ROW_TILE)
            src = pl.ds(seq0 + r, ROW_TILE)
            dst = pl.ds(CONV_PAD + r, ROW_TILE)
            vs_ref[dst, :] = cg_ref[src, :].astype(F32) * xa_ref[src, :].astype(F32)
            zs_ref[dst, :] = ga_ref[src, :].astype(F32) * _sigmoid(gb_ref[src, :].astype(F32))
            return carry

        lax.fori_loop(0, length // ROW_TILE, fill, 0)

        def conv(i, carry):
            r = pl.multiple_of(i * CONV_ROWS, CONV_ROWS)
            win = pl.ds(r, CONV_ROWS + 2 * CONV_PAD)
            out = pl.ds(seq0 + r, CONV_ROWS)
            acc3 = _depthwise_taps(vs_ref[win, :], ca, SC_WIDTH)
            ya_ref[out, :] = (bg_ref[out, :].astype(F32) * acc3).astype(BF16)
            acc = _depthwise_taps(zs_ref[win, :], dw, CF_KERNEL)
            zc_ref[out, :] = (acc + dwb).astype(BF16)
            return carry

        lax.fori_loop(0, length // CONV_ROWS, conv, 0)

    run_sequence(0, CTX)
    run_sequence(CTX, SEQ)


def _conv(u, conv_a, dw_w, dw_b, j):
    nblk = SC_W // CONV_CW
    part = lambda k: pl.BlockSpec((None, T, CONV_CW), lambda b, cb: (b, 0, k * nblk + cb))
    out_spec = pl.BlockSpec((None, T, CONV_CW), lambda b, cb: (b, 0, cb))
    return pl.pallas_call(
        _conv_kernel,
        grid=(B, nblk),
        in_specs=[part(0), part(1), part(2), part(3), part(4),
                  pl.BlockSpec((None, SC_WIDTH, CONV_CW), lambda b, cb: (j, 0, cb)),
                  pl.BlockSpec((None, CF_KERNEL, CONV_CW), lambda b, cb: (j, 0, cb)),
                  pl.BlockSpec((None, 1, CONV_CW), lambda b, cb: (j, 0, cb))],
        out_specs=[out_spec, out_spec],
        out_shape=[jax.ShapeDtypeStruct((B, T, HALF), BF16)] * 2,
        scratch_shapes=[pltpu.VMEM((SEQ + 2 * CONV_PAD, CONV_CW), F32)] * 2,
        compiler_params=pltpu.CompilerParams(
            dimension_semantics=("arbitrary", "arbitrary"), vmem_limit_bytes=VMEM_LIMIT),
        name="conv_mixers",
    )(u, u, u, u, u, conv_a, dw_w, dw_b.reshape(dw_b.shape[0], 1, dw_b.shape[1]))


N_RPB_DR = 2 * NA_ROWS - 1
N_RPB_DC = 2 * NA_COLS - 1
ATT_MASKED = N_RPB_DR


def _att_block_geometry(blk):
    r0 = blk * ATT_QROWS
    return min(max(r0 - NA_ROWS // 2, 0), GRID_ROWS - ATT_KROWS)


def _softmax_pv(parts):
    mx = None
    for s, _ in parts:
        pm = jnp.max(s, axis=-1, keepdims=True)
        mx = pm if mx is None else jnp.maximum(mx, pm)
    den = None
    num = None
    for s, v in parts:
        e = jnp.exp(s - mx)
        ps = jnp.sum(e, axis=-1, keepdims=True)
        pv = jnp.dot(e.astype(BF16), v, preferred_element_type=F32)
        den = ps if den is None else den + ps
        num = pv if num is None else num + pv
    return num / den


def _att_kernel(rpb_ref, q_ref, k_ref, v_ref, o_ref, tile_ref, *, rpb_base):
    @pl.when(pl.program_id(1) == 0)
    def _():
        base = (rpb_base + pl.program_id(0)) * (N_RPB_DR * N_RPB_DC)
        ci = lax.broadcasted_iota(I32, (GRID_W, 2 * GRID_W), 0)
        wi = lax.broadcasted_iota(I32, (GRID_W, 2 * GRID_W), 1) & (GRID_W - 1)
        cs = jnp.clip(ci - NA_COLS // 2, 0, GRID_W - NA_COLS)
        dsel = jnp.where((wi >= cs) & (wi < cs + NA_COLS), wi - ci + (NA_COLS - 1), -1)
        masked = jnp.full((GRID_W, 2 * GRID_W), NEG_INF, F32)
        for dr in range(N_RPB_DR):
            t = masked
            for dc in range(N_RPB_DC):
                t = jnp.where(dsel == dc, rpb_ref[base + dr * N_RPB_DC + dc], t)
            tile_ref[dr] = t
        tile_ref[ATT_MASKED] = masked

    def block_bias(blk, kr0):
        low_half = lax.broadcasted_iota(I32, (GRID_W, 2 * GRID_W), 1) < GRID_W
        bands = []
        for qr in range(ATT_QROWS):
            r = blk * ATT_QROWS + qr
            rs = min(max(r - NA_ROWS // 2, 0), GRID_ROWS - NA_ROWS)
            tiles = []
            for jp in range(ATT_KROWS // 2):
                idx = [kr - r + (NA_ROWS - 1) if rs <= kr < rs + NA_ROWS else ATT_MASKED
                       for kr in (kr0 + 2 * jp, kr0 + 2 * jp + 1)]
                if idx[0] == idx[1]:
                    tiles.append(tile_ref[idx[0]])
                else:
                    tiles.append(jnp.where(low_half, tile_ref[idx[0]], tile_ref[idx[1]]))
            bands.append(jnp.concatenate(tiles, axis=1))
        return jnp.concatenate(bands, axis=0)

    scale = NA_DH ** -0.5
    nt = (((1,), (1,)), ((), ()))
    kc = k_ref[pl.ds(0, CTX), :]
    vc = v_ref[pl.ds(0, CTX), :]

    qc = q_ref[pl.ds(0, CTX), :]
    s_cc = lax.dot_general(qc, kc, nt, preferred_element_type=F32) * scale
    o_ref[pl.ds(0, CTX), :] = _softmax_pv([(s_cc, vc)]).astype(BF16)

    for blk in range(GRID_ROWS // ATT_QROWS):
        kr0 = _att_block_geometry(blk)
        qrows = pl.ds(CTX + blk * ATT_Q, ATT_Q)
        krows = pl.ds(CTX + kr0 * GRID_W, ATT_K)
        qb = q_ref[qrows, :]
        s_w = lax.dot_general(qb, k_ref[krows, :], nt, preferred_element_type=F32) * scale
        s_w = s_w + block_bias(blk, kr0)
        s_c = lax.dot_general(qb, kc, nt, preferred_element_type=F32) * scale
        o_ref[qrows, :] = _softmax_pv([(s_w, v_ref[krows, :]), (s_c, vc)]).astype(BF16)


def _attention(u, rpb_flat, j):
    col = lambda off: pl.BlockSpec((None, T, NA_DH), lambda h, b: (b, 0, off + h))
    return pl.pallas_call(
        functools.partial(_att_kernel, rpb_base=j * NA_HEADS),
        grid=(NA_HEADS, B),
        in_specs=[pl.BlockSpec(memory_space=pltpu.SMEM),
                  col(0), col(NA_HEADS), col(2 * NA_HEADS)],
        out_specs=pl.BlockSpec((None, T, NA_DH), lambda h, b: (b, 0, h)),
        out_shape=jax.ShapeDtypeStruct((B, T, HALF), BF16),
        scratch_shapes=[pltpu.VMEM((N_RPB_DR + 1, GRID_W, 2 * GRID_W), F32)],
        compiler_params=pltpu.CompilerParams(
            dimension_semantics=("arbitrary", "arbitrary"), vmem_limit_bytes=VMEM_LIMIT),
        name="neighbourhood_attention",
    )(rpb_flat, u, u, u)


POOL_KWIN = 512


def _pool_kernel(p_ref, pw_ref, ps_ref, o_ref):
    half = lax.shift_left(jnp.int32(1), pl.program_id(1))
    pw = pw_ref[...].astype(BF16)
    sc = ps_ref[...]

    def tile(row0, seq0, length, ks, kwin):
        t = (row0 - seq0) + lax.broadcasted_iota(I32, (ROW_TILE, 1), 0)
        s = ks + lax.broadcasted_iota(I32, (1, kwin), 1)
        lo = t - half
        hi = t + half - 1
        band = jnp.where((s >= lo) & (s <= hi), 1.0, 0.0).astype(BF16)
        ssum = jnp.dot(band, p_ref[pl.ds(seq0 + ks, kwin), :], preferred_element_type=F32)
        cnt = (jnp.minimum(hi, length - 1) - jnp.maximum(lo, 0) + 1).astype(F32)
        mix = ssum / cnt - p_ref[pl.ds(row0, ROW_TILE), :].astype(F32)
        y = jnp.dot(mix.astype(BF16), pw, preferred_element_type=F32) * sc
        o_ref[pl.ds(row0, ROW_TILE), :] = y.astype(BF16)

    tile(0, 0, CTX, 0, CTX)
    for jt in range(SEQ // ROW_TILE):
        ks = min(max(ROW_TILE * jt - (POOL_KWIN - ROW_TILE) // 2, 0), SEQ - POOL_KWIN)
        tile(CTX + ROW_TILE * jt, CTX, SEQ, ks, POOL_KWIN)


def _pool(u, pool_w, pool_scale, j):
    p_block0 = 3 * HALF // POOL_GC
    return pl.pallas_call(
        _pool_kernel,
        grid=(B, N_POOL),
        in_specs=[pl.BlockSpec((None, T, POOL_GC), lambda b, g: (b, 0, p_block0 + g)),
                  pl.BlockSpec((None, None, POOL_GC, POOL_GC), lambda b, g: (j, g, 0, 0)),
                  pl.BlockSpec((None, None, 1, POOL_GC), lambda b, g: (j, g, 0, 0))],
        out_specs=pl.BlockSpec((None, T, POOL_GC), lambda b, g: (b, 0, g)),
        out_shape=jax.ShapeDtypeStruct((B, T, HALF), BF16),
        compiler_params=pltpu.CompilerParams(
            dimension_semantics=("arbitrary", "arbitrary"), vmem_limit_bytes=VMEM_LIMIT),
        name="pool_mixer",
    )(u, pool_w, pool_scale.reshape(pool_scale.shape[0], N_POOL, 1, POOL_GC))


ROUTE_LANES = 128


def _route(logits):
    lane = lax.broadcasted_iota(I32, logits.shape, 1)
    lanef = lane.astype(F32)
    is_g = lane < N_GROUPS
    gl = jnp.where(is_g, logits, NEG_INF)
    ge = jnp.where(is_g, jnp.exp(gl - jnp.max(gl, axis=-1, keepdims=True)), 0.0)
    gp = ge / jnp.sum(ge, axis=-1, keepdims=True)
    g_p = jnp.max(gp, axis=-1, keepdims=True)
    g_idx = jnp.min(jnp.where(is_g & (gp == g_p), lanef, float(ROUTE_LANES)), axis=-1, keepdims=True)
    first = float(N_GROUPS) + float(EPG) * g_idx
    sel = (lanef >= first) & (lanef < first + float(EPG))
    el = jnp.where(sel, logits, NEG_INF)
    ee = jnp.where(sel, jnp.exp(el - jnp.max(el, axis=-1, keepdims=True)), 0.0)
    ep = jnp.where(sel, ee / jnp.sum(ee, axis=-1, keepdims=True), -1.0)
    p1 = jnp.max(ep, axis=-1, keepdims=True)
    i1 = jnp.min(jnp.where(ep == p1, lanef, float(ROUTE_LANES)), axis=-1, keepdims=True)
    ep2 = jnp.where(lanef == i1, -1.0, ep)
    p2 = jnp.max(ep2, axis=-1, keepdims=True)
    i2 = jnp.min(jnp.where(ep2 == p2, lanef, float(ROUTE_LANES)), axis=-1, keepdims=True)
    psum = p1 + p2
    gate1 = g_p * p1 / psum
    gate2 = g_p * p2 / psum
    return jnp.where(lane == 0, i1 - float(N_GROUPS),
                     jnp.where(lane == 1, i2 - float(N_GROUPS),
                               jnp.where(lane == 2, gate1,
                                         jnp.where(lane == 3, gate2, 0.0))))


N_ROW_TILES = NTOK // ROW_TILE
TILES_PER_BATCH = T // ROW_TILE


def _proj_kernel(ya_ref, yb_ref, w_ref, lng_ref, lnb_ref, x_ref, modb_ref, modc_ref, g2_ref,
                 rwh_ref, rwl_ref, rb_ref, xo_ref, hp_ref, ri_ref,
                 lhs_s, out_a, out_b, hh_s, hl_s, vec_s, *, even):
    s = pl.program_id(0)
    pieces = ROW_TILE // NM_CHUNK

    @pl.when(s == 0)
    def _():
        out_b[...] = jnp.zeros_like(out_b)

    is_ctx = jnp.maximum(s - 1, 0) % TILES_PER_BATCH == 0
    mod = jnp.where(is_ctx, modc_ref[...], modb_ref[...])
    vec_s[0:1, :] = mod[2:3, :]
    vec_s[1:2, :] = g2_ref[...] * (1.0 + mod[4:5, :])
    vec_s[2:3, :] = mod[3:4, :]

    def step(mm_out, epi_in):
        for c in range(pieces):
            rows = pl.ds(c * NM_CHUNK, NM_CHUNK)
            xn = x_ref[rows, :] + vec_s[0:1, :] * epi_in[rows, :]
            xo_ref[rows, :] = xn
            inv = lax.rsqrt(jnp.mean(xn * xn, axis=-1, keepdims=True) + RMS_EPS)
            h2 = xn * inv * vec_s[1:2, :] + vec_s[2:3, :]
            hp_ref[rows, :] = h2
            hb = h2.astype(BF16)
            hh_s[rows, :] = hb
            hl_s[rows, :] = (h2 - hb.astype(F32)).astype(BF16)
        hh = hh_s[...]
        logits = (jnp.dot(hh, rwh_ref[...], preferred_element_type=F32)
                  + jnp.dot(hl_s[...], rwh_ref[...], preferred_element_type=F32)
                  + jnp.dot(hh, rwl_ref[...], preferred_element_type=F32)) + rb_ref[...]
        ri_ref[...] = _route(logits)

        if even:
            for c in range(pieces):
                rows = pl.ds(c * NM_CHUNK, NM_CHUNK)
                zc = yb_ref[rows, :].astype(F32)
                dz = zc - jnp.mean(zc, axis=-1, keepdims=True)
                var = jnp.mean(dz * dz, axis=-1, keepdims=True)
                z = dz * lax.rsqrt(var + LN_EPS) * lng_ref[...] + lnb_ref[...]
                lhs_s[rows, pl.ds(HALF, HALF)] = (z * _sigmoid(z)).astype(BF16)
        else:
            lhs_s[:, pl.ds(HALF, HALF)] = yb_ref[...]
        lhs_s[:, pl.ds(0, HALF)] = ya_ref[...]
        for n in range(D // NM_TN):
            cols = pl.ds(n * NM_TN, NM_TN)
            mm_out[:, cols] = jnp.dot(lhs_s[...], w_ref[:, cols], preferred_element_type=F32)

    @pl.when(s % 2 == 0)
    def _():
        step(out_a, out_b)

    @pl.when(s % 2 == 1)
    def _():
        step(out_b, out_a)


def _proj(ya, yb, w_bf, widx, ln_g, ln_b, xc, mods, layer, g2, rw_hi, rw_lo, rb, even):
    def tile_of(t):
        return t // TILES_PER_BATCH, t % TILES_PER_BATCH

    def ahead(width):
        return pl.BlockSpec((None, ROW_TILE, width),
                            lambda s: (*tile_of(jnp.minimum(s, N_ROW_TILES - 1)), 0))

    def behind(width):
        return pl.BlockSpec((None, ROW_TILE, width), lambda s: (*tile_of(jnp.maximum(s - 1, 0)), 0))

    const = lambda *shape: pl.BlockSpec((None,) + shape, lambda s: (widx,) + (0,) * len(shape))
    per_layer = lambda *shape: pl.BlockSpec((None,) + shape, lambda s: (layer,) + (0,) * len(shape))
    return pl.pallas_call(
        functools.partial(_proj_kernel, even=even),
        grid=(N_ROW_TILES + 1,),
        in_specs=[ahead(HALF), ahead(HALF),
                  const(D, D), const(1, HALF), const(1, HALF),
                  behind(D),
                  pl.BlockSpec((None, None, 6, D),
                               lambda s: (layer, jnp.maximum(s - 1, 0) // TILES_PER_BATCH, 0, 0)),
                  pl.BlockSpec((None, None, 6, D), lambda s: (layer, B, 0, 0)),
                  per_layer(1, D), per_layer(D, ROUTE_LANES), per_layer(D, ROUTE_LANES),
                  per_layer(1, ROUTE_LANES)],
        out_specs=[behind(D), behind(D), behind(ROUTE_LANES)],
        out_shape=[jax.ShapeDtypeStruct((B, T, D), F32),
                   jax.ShapeDtypeStruct((B, T, D), F32),
                   jax.ShapeDtypeStruct((B, T, ROUTE_LANES), F32)],
        scratch_shapes=[pltpu.VMEM((ROW_TILE, D), BF16),
                        pltpu.VMEM((ROW_TILE, D), F32),
                        pltpu.VMEM((ROW_TILE, D), F32),
                        pltpu.VMEM((ROW_TILE, D), BF16),
                        pltpu.VMEM((ROW_TILE, D), BF16),
                        pltpu.VMEM((8, D), F32)],
        compiler_params=pltpu.CompilerParams(
            dimension_semantics=("arbitrary",), vmem_limit_bytes=VMEM_LIMIT),
        name="out_proj_router",
    )(ya, yb, w_bf, ln_g, ln_b, xc, mods, mods, g2, rw_hi, rw_lo, rb)


def _positions_kernel(ri_ref, pos_ref, base_ref):
    ntiles = NTOK // ROW_TILE
    lanef = lax.broadcasted_iota(I32, (ROW_TILE, ROUTE_LANES), 1).astype(F32)

    def count(i, acc):
        r = ri_ref[pl.ds(pl.multiple_of(i * ROW_TILE, ROW_TILE), ROW_TILE), :]
        hit = (lanef == r[:, 0:1]) | (lanef == r[:, 1:2])
        return acc + jnp.sum(jnp.where(hit, 1.0, 0.0), axis=0, keepdims=True)

    cnt = lax.fori_loop(0, ntiles, count, jnp.zeros((1, ROUTE_LANES), F32))
    ri = lax.broadcasted_iota(I32, (ROUTE_LANES, ROUTE_LANES), 0)
    ci = lax.broadcasted_iota(I32, (ROUTE_LANES, ROUTE_LANES), 1)
    before = jnp.where(ri < ci, 1.0, 0.0)
    cnt8 = jnp.broadcast_to(cnt, (8, ROUTE_LANES))
    base8 = jnp.dot(cnt8, before, preferred_element_type=F32, precision=HIGHEST)
    base_ref[...] = base8.astype(I32)
    base_col = jnp.broadcast_to(base8[0:1, :], (ROUTE_LANES, ROUTE_LANES)).T[:, 0:1]

    ti = lax.broadcasted_iota(I32, (ROW_TILE, ROW_TILE), 0)
    tj = lax.broadcasted_iota(I32, (ROW_TILE, ROW_TILE), 1)
    earlier = jnp.where(ti < tj, 1.0, 0.0).astype(BF16)
    subf = lax.broadcasted_iota(I32, (ROUTE_LANES, ROW_TILE), 0).astype(F32)

    def place(i, carry):
        start = pl.multiple_of(i * ROW_TILE, ROW_TILE)
        rt = ri_ref[pl.ds(start, ROW_TILE), :].T
        e0 = rt[0:1, :]
        e1 = rt[1:2, :]
        hit0 = subf == e0
        hit1 = subf == e1
        hits = jnp.where(hit0 | hit1, 1.0, 0.0)
        rank = jnp.dot(hits.astype(BF16), earlier, preferred_element_type=F32) + carry + base_col
        pos_ref[0:1, pl.ds(start, ROW_TILE)] = jnp.sum(
            jnp.where(hit0, rank, 0.0), axis=0, keepdims=True).astype(I32)
        pos_ref[1:2, pl.ds(start, ROW_TILE)] = jnp.sum(
            jnp.where(hit1, rank, 0.0), axis=0, keepdims=True).astype(I32)
        return carry + jnp.sum(hits, axis=1, keepdims=True)

    lax.fori_loop(0, ntiles, place, jnp.zeros((ROUTE_LANES, 1), F32))


def _positions(ri_flat):
    return pl.pallas_call(
        _positions_kernel,
        out_shape=[jax.ShapeDtypeStruct((2, NTOK), I32),
                   jax.ShapeDtypeStruct((8, ROUTE_LANES), I32)],
        compiler_params=pltpu.CompilerParams(vmem_limit_bytes=VMEM_LIMIT),
        name="moe_positions",
    )(ri_flat)


INVERT_UNROLL = 8


def _items_kernel(base_ref, pos_ref, it_ref, src_ref):
    def init(i, carry):
        it_ref[ITEM_NEW + i] = 0
        it_ref[ITEM_SLOT + i] = 0
        it_ref[ITEM_NEXT + i] = -1
        return carry

    lax.fori_loop(0, N_ITEMS, init, 0)

    def per_expert(e, carry):
        n, run, run_start = carry
        s = base_ref[e]
        t = base_ref[e + 1]
        first = lax.shift_right_logical(s, 8)
        last = lax.shift_right_logical(jnp.maximum(t - 1, 0), 8)
        used = t > s
        ntile = jnp.where(used, last - first + 1, 0)

        def per_tile(k, n2):
            tile = first + k
            it_ref[n2] = tile
            it_ref[N_ITEMS + n2] = e
            it_ref[2 * N_ITEMS + n2] = jnp.maximum(s - tile * ROW_TILE, 0)
            it_ref[3 * N_ITEMS + n2] = jnp.minimum(t - tile * ROW_TILE, ROW_TILE)
            return n2 + 1

        n_end = lax.fori_loop(0, ntile, per_tile, n)
        it_ref[ITEM_NEW + n] = jnp.where(used, 1, it_ref[ITEM_NEW + n])
        it_ref[ITEM_SLOT + n] = jnp.where(used, run & 1, it_ref[ITEM_SLOT + n])
        link = used & (run > 0)
        it_ref[ITEM_NEXT + run_start] = jnp.where(link, e, it_ref[ITEM_NEXT + run_start])
        return n_end, run + jnp.where(used, 1, 0), jnp.where(used, n, run_start)

    n_used, _, _ = lax.fori_loop(0, N_EXPERTS, per_expert,
                                 (jnp.int32(0), jnp.int32(0), jnp.int32(0)))
    last_e = it_ref[N_ITEMS + n_used - 1]

    def pad(i, carry):
        it_ref[i] = N_SORT_TILES - 1
        it_ref[N_ITEMS + i] = last_e
        it_ref[2 * N_ITEMS + i] = 0
        it_ref[3 * N_ITEMS + i] = 0
        return carry

    lax.fori_loop(n_used, N_ITEMS, pad, 0)

    def invert(g, carry):
        for k in range(INVERT_UNROLL):
            t = g * INVERT_UNROLL + k
            src_ref[pos_ref[t]] = t
            src_ref[pos_ref[NTOK + t]] = t
        return carry

    lax.fori_loop(0, NTOK // INVERT_UNROLL, invert, 0)


def _items(base_flat, pos_flat):
    smem = pl.BlockSpec(memory_space=pltpu.SMEM)
    return pl.pallas_call(
        _items_kernel,
        in_specs=[smem, smem],
        out_specs=[smem, smem],
        out_shape=[jax.ShapeDtypeStruct((7 * N_ITEMS,), I32),
                   jax.ShapeDtypeStruct((NSLOT,), I32)],
        name="moe_items",
    )(base_flat, pos_flat)


ROW_DMA_UNROLL = 8


def _gmm_kernel(it_ref, src_ref, h_hbm, w1_hbm, w3_hbm, w2_hbm, o_ref,
                xbuf, w1_raw, w3_raw, w2_raw, w1_bf, w3_bf, w2_bf, sems, row_sems, *, layer):
    i = pl.program_id(0)
    tile = it_ref[i]
    par = tile & 1

    def gather_rows(tl, p, unroll):
        def rows(g, carry):
            for k in range(unroll):
                r = g * unroll + k
                pltpu.make_async_copy(h_hbm.at[pl.ds(src_ref[tl * ROW_TILE + r], 1)],
                                      xbuf.at[p, pl.ds(r, 1)], row_sems.at[p]).start()
            return carry

        if unroll == ROW_TILE:
            rows(0, 0)
        else:
            lax.fori_loop(0, ROW_TILE // unroll, rows, 0)
    expert = it_ref[N_ITEMS + i]
    lo = it_ref[2 * N_ITEMS + i]
    hi = it_ref[3 * N_ITEMS + i]
    slot = it_ref[ITEM_SLOT + i]
    nxt = it_ref[ITEM_NEXT + i]
    prev_tile = it_ref[jnp.maximum(i - 1, 0)]

    def weight_copies(e, s):
        return (pltpu.make_async_copy(w1_hbm.at[layer, e], w1_raw.at[s], sems.at[s, 0]),
                pltpu.make_async_copy(w3_hbm.at[layer, e], w3_raw.at[s], sems.at[s, 1]),
                pltpu.make_async_copy(w2_hbm.at[layer, e], w2_raw.at[s], sems.at[s, 2]))

    @pl.when(it_ref[ITEM_NEW + i] == 1)
    def _():
        @pl.when(i == 0)
        def _():
            for cp in weight_copies(expert, slot):
                cp.start()

        for cp in weight_copies(expert, slot):
            cp.wait()

        @pl.when(nxt >= 0)
        def _():
            for cp in weight_copies(nxt, 1 - slot):
                cp.start()

        w1_bf[...] = w1_raw[slot].astype(BF16)
        w3_bf[...] = w3_raw[slot].astype(BF16)
        w2_bf[...] = w2_raw[slot].astype(BF16)

    @pl.when((i == 0) | (tile != prev_tile))
    def _():
        @pl.when(i == 0)
        def _():
            gather_rows(tile, par, ROW_DMA_UNROLL)

        pltpu.make_async_copy(xbuf.at[par], xbuf.at[par], row_sems.at[par]).wait()

        @pl.when(tile + 1 < N_SORT_TILES)
        def _():
            gather_rows(tile + 1, 1 - par, ROW_TILE)

        o_ref[...] = jnp.zeros_like(o_ref)

    @pl.when(hi > lo)
    def _():
        xb = xbuf[par].astype(BF16)
        a = jnp.dot(xb, w1_bf[...], preferred_element_type=F32)
        b = jnp.dot(xb, w3_bf[...], preferred_element_type=F32)
        rows = lax.broadcasted_iota(I32, (ROW_TILE, 1), 0)
        hid = jnp.where((rows >= lo) & (rows < hi), a * _sigmoid(a) * b, 0.0)
        o_ref[...] += jnp.dot(hid.astype(BF16), w2_bf[...], preferred_element_type=F32)


def _gmm(items, src, h_flat, w1, w3, w2, layer):
    hbm = pl.BlockSpec(memory_space=pl.ANY)
    return pl.pallas_call(
        functools.partial(_gmm_kernel, layer=layer),
        grid_spec=pltpu.PrefetchScalarGridSpec(
            num_scalar_prefetch=2,
            grid=(N_ITEMS,),
            in_specs=[hbm, hbm, hbm, hbm],
            out_specs=pl.BlockSpec((ROW_TILE, D), lambda i, it, src: (it[i], 0)),
            scratch_shapes=[pltpu.VMEM((2, ROW_TILE, D), F32),
                            pltpu.VMEM((2, D, EXPERT_FF), F32),
                            pltpu.VMEM((2, D, EXPERT_FF), F32),
                            pltpu.VMEM((2, EXPERT_FF, D), F32),
                            pltpu.VMEM((D, EXPERT_FF), BF16),
                            pltpu.VMEM((D, EXPERT_FF), BF16),
                            pltpu.VMEM((EXPERT_FF, D), BF16),
                            pltpu.SemaphoreType.DMA((2, 3)),
                            pltpu.SemaphoreType.DMA((2,))],
        ),
        out_shape=jax.ShapeDtypeStruct((NSLOT, D), F32),
        compiler_params=pltpu.CompilerParams(
            dimension_semantics=("arbitrary",), vmem_limit_bytes=VMEM_LIMIT),
        name="moe_experts",
    )(items, src, h_flat, w1, w3, w2)


def _combine_kernel(pos_ref, ys_hbm, x_ref, ri_ref, modb_ref, modc_ref, fg_ref, o_ref, buf, gate_s, sems,
                    *, final, tiles_per_b, m_off):
    i = pl.program_id(0)
    ntiles = pl.num_programs(0)

    def issue(tile_i, par, unroll):
        t0 = (tile_i // tiles_per_b) * T + (tile_i % tiles_per_b + m_off) * ROW_TILE

        def rows(g, carry):
            for k in range(unroll):
                r = g * unroll + k
                for s in range(2):
                    pltpu.make_async_copy(ys_hbm.at[pl.ds(pos_ref[s * NTOK + t0 + r], 1)],
                                          buf.at[par, s, pl.ds(r, 1)], sems.at[par]).start()
            return carry

        if unroll == ROW_TILE:
            rows(0, 0)
        else:
            lax.fori_loop(0, ROW_TILE // unroll, rows, 0)

    @pl.when(i == 0)
    def _():
        issue(0, 0, ROW_DMA_UNROLL)

    @pl.when(i + 1 < ntiles)
    def _():
        issue(i + 1, (i + 1) % 2, ROW_TILE)

    par = i % 2
    pltpu.make_async_copy(buf.at[par], buf.at[par], sems.at[par]).wait()
    m = i % tiles_per_b + m_off
    gate_s[...] = jnp.where(m == 0, modc_ref[5:6, :], modb_ref[5:6, :])
    for c in range(ROW_TILE // NM_CHUNK):
        rows = pl.ds(c * NM_CHUNK, NM_CHUNK)
        g0 = ri_ref[rows, 2:3]
        g1 = ri_ref[rows, 3:4]
        xn = x_ref[rows, :] + gate_s[...] * (g0 * buf[par, 0, rows, :] + g1 * buf[par, 1, rows, :])
        if final:
            inv = lax.rsqrt(jnp.mean(xn * xn, axis=-1, keepdims=True) + RMS_EPS)
            xn = xn * inv * fg_ref[...]
        o_ref[rows, :] = xn


def _combine(pos_flat, ys, xc, ri, mods, layer, final_g, final):
    if final:
        tiles_per_b, m_off, rows_out = SEQ // ROW_TILE, CTX // ROW_TILE, SEQ
    else:
        tiles_per_b, m_off, rows_out = T // ROW_TILE, 0, T
    tok = lambda width: pl.BlockSpec(
        (None, ROW_TILE, width), lambda i, pos: (i // tiles_per_b, i % tiles_per_b + m_off, 0))
    return pl.pallas_call(
        functools.partial(_combine_kernel, final=final, tiles_per_b=tiles_per_b, m_off=m_off),
        grid_spec=pltpu.PrefetchScalarGridSpec(
            num_scalar_prefetch=1,
            grid=(B * tiles_per_b,),
            in_specs=[
                pl.BlockSpec(memory_space=pl.ANY),
                tok(D),
                tok(ROUTE_LANES),
                pl.BlockSpec((None, None, 6, D), lambda i, pos: (layer, i // tiles_per_b, 0, 0)),
                pl.BlockSpec((None, None, 6, D), lambda i, pos: (layer, B, 0, 0)),
                pl.BlockSpec((1, D), lambda i, pos: (0, 0)),
            ],
            out_specs=pl.BlockSpec((None, ROW_TILE, D),
                                   lambda i, pos: (i // tiles_per_b, i % tiles_per_b, 0)),
            scratch_shapes=[pltpu.VMEM((2, 2, ROW_TILE, D), F32),
                            pltpu.VMEM((1, D), F32),
                            pltpu.SemaphoreType.DMA((2,))],
        ),
        out_shape=jax.ShapeDtypeStruct((B, rows_out, D), F32),
        compiler_params=pltpu.CompilerParams(
            dimension_semantics=("arbitrary",), vmem_limit_bytes=VMEM_LIMIT),
        name="moe_combine",
    )(pos_flat, ys, xc, ri, mods, mods, final_g)


def kernel(x, c, ctx, c_ctx, ada_w, ada_b, norm1_g, norm2_g, ev_w_in, ev_conv_a, ev_dw_w, ev_dw_b,
           ev_ln_g, ev_ln_b, ev_w_out, od_w_in, od_rpb, od_pool_w, od_pool_scale, od_w_out,
           moe_rg_w, moe_rg_b, moe_re_w, moe_re_b, moe_w1, moe_w3, moe_w2, final_g):
    xc = jnp.concatenate([ctx, x], axis=1)
    c8 = jnp.concatenate([c, c_ctx[None, :], jnp.zeros((8 - B - 1, D), F32)], axis=0)
    mods = _ada(c8, ada_w, ada_b).reshape(DEPTH, 8, 6, D)

    ev_w_in_bf = ev_w_in.astype(BF16)
    od_w_in_bf = od_w_in.astype(BF16)
    ev_w_out_bf = ev_w_out.astype(BF16)
    od_w_out_bf = od_w_out.astype(BF16)
    pad = ROUTE_LANES - N_GROUPS - N_EXPERTS
    rw = jnp.concatenate([moe_rg_w, moe_re_w, jnp.zeros((DEPTH, D, pad), F32)], axis=-1)
    rw_hi = rw.astype(BF16)
    rw_lo = (rw - rw_hi.astype(F32)).astype(BF16)
    rb = jnp.concatenate([moe_rg_b, moe_re_b, jnp.zeros((DEPTH, pad), F32)],
                         axis=-1).reshape(DEPTH, 1, ROUTE_LANES)
    final_g2 = final_g.reshape(1, D)
    norm1_g = norm1_g.reshape(DEPTH, 1, D)
    norm2_g = norm2_g.reshape(DEPTH, 1, D)
    ev_ln_g = ev_ln_g.reshape(-1, 1, HALF)
    ev_ln_b = ev_ln_b.reshape(-1, 1, HALF)

    for i in range(DEPTH):
        j = i // 2
        last = i == DEPTH - 1
        if i % 2 == 1:
            u = _norm_mm(xc, mods, i, norm1_g, od_w_in_bf, j)
            ya = _attention(u, od_rpb.reshape(-1), j)
            yb = _pool(u, od_pool_w, od_pool_scale, j)
            xc, hp, ri = _proj(ya, yb, od_w_out_bf, j, ev_ln_g, ev_ln_b, xc, mods, i, norm2_g,
                               rw_hi, rw_lo, rb, even=False)
        else:
            u = _norm_mm(xc, mods, i, norm1_g, ev_w_in_bf, j)
            ya, yb = _conv(u, ev_conv_a, ev_dw_w, ev_dw_b, j)
            xc, hp, ri = _proj(ya, yb, ev_w_out_bf, j, ev_ln_g, ev_ln_b, xc, mods, i, norm2_g,
                               rw_hi, rw_lo, rb, even=True)
        pos, base = _positions(ri.reshape(NTOK, ROUTE_LANES))
        pos_flat = pos.reshape(2 * NTOK)
        items, src = _items(base[0, :N_EXPERTS + 1 + 7], pos_flat)
        ys = _gmm(items, src, hp.reshape(NTOK, D), moe_w1, moe_w3, moe_w2, i)
        xc = _combine(pos_flat, ys, xc, ri, mods, i, final_g2, last)
    return xc
```

```python
import functools

import numpy as np
import jax
import jax.numpy as jnp
from jax import lax
from jax.experimental import pallas as pl
from jax.experimental.pallas import tpu as pltpu

F32 = jnp.float32
BF16 = jnp.bfloat16
I32 = jnp.int32
HIGHEST = lax.Precision.HIGHEST

D = 2048
B = 4
SEQ = 2048
CTX = 256
T = CTX + SEQ
NTOK = B * T
DEPTH = 4
GRID_W = 64
GRID_ROWS = SEQ // GRID_W
HALF = D // 2
SC_W = 1024
CF_KERNEL = 31
SC_WIDTH = 3
NA_HEADS = 8
NA_DH = 128
NA_ROWS = 8
NA_COLS = 16
N_POOL = 4
POOL_GC = 256
N_EXPERTS = 32
N_GROUPS = 4
EPG = 8
EXPERT_FF = 256
RMS_EPS = 1e-6
LN_EPS = 1e-5
NEG_INF = -1e30

ROW_TILE = 256
NSLOT = 2 * NTOK
N_SORT_TILES = NSLOT // ROW_TILE
N_ITEMS = 104
ITEM_NEW, ITEM_SLOT, ITEM_NEXT = 4 * N_ITEMS, 5 * N_ITEMS, 6 * N_ITEMS
NM_TN = 1024
NM_CHUNK = 16
ATT_QROWS = 4
ATT_Q = ATT_QROWS * GRID_W
ATT_KROWS = 12
ATT_K = ATT_KROWS * GRID_W
CONV_CW = 256
CONV_PAD = 16
CONV_ROWS = 32
VMEM_LIMIT = 56 * 1024 * 1024


def _sigmoid(v):
    return 1.0 / (1.0 + jnp.exp(-v))


ADA_TN = 1024


def _ada_kernel(c_ref, w_ref, b_ref, o_ref):
    cv = c_ref[...]
    s = cv * _sigmoid(cv)
    s_hi = s.astype(BF16)
    s_lo = (s - s_hi.astype(F32)).astype(BF16)
    w = w_ref[0]
    w_hi = w.astype(BF16)
    w_lo = (w - w_hi.astype(F32)).astype(BF16)
    o_ref[0] = (jnp.dot(s_hi, w_hi, preferred_element_type=F32)
                + jnp.dot(s_lo, w_hi, preferred_element_type=F32)
                + jnp.dot(s_hi, w_lo, preferred_element_type=F32)) + b_ref[0]


def _ada(c8, ada_w, ada_b):
    n = ada_w.shape[-1]
    return pl.pallas_call(
        _ada_kernel,
        grid=(DEPTH, n // ADA_TN),
        in_specs=[
            pl.BlockSpec((8, D), lambda l, j: (0, 0)),
            pl.BlockSpec((1, D, ADA_TN), lambda l, j: (l, 0, j)),
            pl.BlockSpec((1, 1, ADA_TN), lambda l, j: (l, 0, j)),
        ],
        out_specs=pl.BlockSpec((1, 8, ADA_TN), lambda l, j: (l, 0, j)),
        out_shape=jax.ShapeDtypeStruct((DEPTH, 8, n), F32),
        compiler_params=pltpu.CompilerParams(
            dimension_semantics=("arbitrary", "arbitrary"), vmem_limit_bytes=VMEM_LIMIT),
        name="ada",
    )(c8, ada_w, ada_b.reshape(DEPTH, 1, n))


def _norm_mm_kernel(x_ref, modb_ref, modc_ref, g_ref, w_ref, o_ref, h_ref, gain_ref, shift_ref):
    is_ctx = pl.program_id(1) == 0
    gain_ref[...] = g_ref[...] * (1.0 + jnp.where(is_ctx, modc_ref[1:2, :], modb_ref[1:2, :]))
    shift_ref[...] = jnp.where(is_ctx, modc_ref[0:1, :], modb_ref[0:1, :])
    for c in range(ROW_TILE // NM_CHUNK):
        rows = pl.ds(c * NM_CHUNK, NM_CHUNK)
        xv = x_ref[rows, :]
        inv = lax.rsqrt(jnp.mean(xv * xv, axis=-1, keepdims=True) + RMS_EPS)
        h_ref[rows, :] = (xv * inv * gain_ref[...] + shift_ref[...]).astype(BF16)
    for n in range(o_ref.shape[1] // NM_TN):
        cols = pl.ds(n * NM_TN, NM_TN)
        o_ref[:, cols] = jnp.dot(h_ref[...], w_ref[:, cols],
                                 preferred_element_type=F32).astype(BF16)


def _norm_mm(xc, mods, layer, g, w_bf, widx):
    n_out = w_bf.shape[-1]
    return pl.pallas_call(
        _norm_mm_kernel,
        grid=(B, T // ROW_TILE),
        in_specs=[
            pl.BlockSpec((None, ROW_TILE, D), lambda b, m: (b, m, 0)),
            pl.BlockSpec((None, None, 6, D), lambda b, m: (layer, b, 0, 0)),
            pl.BlockSpec((None, None, 6, D), lambda b, m: (layer, B, 0, 0)),
            pl.BlockSpec((None, 1, D), lambda b, m: (layer, 0, 0)),
            pl.BlockSpec((None, D, n_out), lambda b, m: (widx, 0, 0), pipeline_mode=pl.Buffered(1)),
        ],
        out_specs=pl.BlockSpec((None, ROW_TILE, n_out), lambda b, m: (b, m, 0)),
        out_shape=jax.ShapeDtypeStruct((B, T, n_out), BF16),
        scratch_shapes=[pltpu.VMEM((ROW_TILE, D), BF16),
                        pltpu.VMEM((1, D), F32), pltpu.VMEM((1, D), F32)],
        compiler_params=pltpu.CompilerParams(
            dimension_semantics=("arbitrary", "arbitrary"), vmem_limit_bytes=VMEM_LIMIT),
        name="norm_mm",
    )(xc, mods, mods, g, w_bf)


def _depthwise_taps(win, taps, width):
    nwin = CONV_ROWS + 2 * CONV_PAD
    acc = jnp.zeros((CONV_ROWS, win.shape[1]), F32)
    for res in range(8):
        offsets = [o for o in range(res, nwin - CONV_ROWS + 1, 8)
                   if 0 <= o - CONV_PAD + width // 2 < width]
        if not offsets:
            continue
        rolled = win if res == 0 else pltpu.roll(win, nwin - res, 0)
        for o in offsets:
            k = o - CONV_PAD + width // 2
            acc = acc + rolled[o - res:o - res + CONV_ROWS, :] * taps[k:k + 1, :]
    return acc


def _conv_kernel(bg_ref, cg_ref, xa_ref, ga_ref, gb_ref, ca_ref, dw_ref, dwb_ref,
                 ya_ref, zc_ref, vs_ref, zs_ref):
    ca = ca_ref[...]
    dw = dw_ref[...]
    dwb = dwb_ref[...]
    zeros_pad = jnp.zeros((CONV_PAD, CONV_CW), F32)

    def run_sequence(seq0, length):
        vs_ref[pl.ds(0, CONV_PAD), :] = zeros_pad
        zs_ref[pl.ds(0, CONV_PAD), :] = zeros_pad
        vs_ref[pl.ds(CONV_PAD + length, CONV_PAD), :] = zeros_pad
        zs_ref[pl.ds(CONV_PAD + length, CONV_PAD), :] = zeros_pad

        def fill(i, carry):
            r = pl.multiple_of(i * ROW_TILE, ROW_TILE)
            src = pl.ds(seq0 + r, ROW_TILE)
            dst = pl.ds(CONV_PAD + r, ROW_TILE)
            vs_ref[dst, :] = cg_ref[src, :].astype(F32) * xa_ref[src, :].astype(F32)
            zs_ref[dst, :] = ga_ref[src, :].astype(F32) * _sigmoid(gb_ref[src, :].astype(F32))
            return carry

        lax.fori_loop(0, length // ROW_TILE, fill, 0)

        def conv(i, carry):
            r = pl.multiple_of(i * CONV_ROWS, CONV_ROWS)
            win = pl.ds(r, CONV_ROWS + 2 * CONV_PAD)
            out = pl.ds(seq0 + r, CONV_ROWS)
            acc3 = _depthwise_taps(vs_ref[win, :], ca, SC_WIDTH)
            ya_ref[out, :] = (bg_ref[out, :].astype(F32) * acc3).astype(BF16)
            acc = _depthwise_taps(zs_ref[win, :], dw, CF_KERNEL)
            zc_ref[out, :] = (acc + dwb).astype(BF16)
            return carry

        lax.fori_loop(0, length // CONV_ROWS, conv, 0)

    run_sequence(0, CTX)
    run_sequence(CTX, SEQ)


def _conv(u, conv_a, dw_w, dw_b, j):
    nblk = SC_W // CONV_CW
    part = lambda k: pl.BlockSpec((None, T, CONV_CW), lambda b, cb: (b, 0, k * nblk + cb))
    out_spec = pl.BlockSpec((None, T, CONV_CW), lambda b, cb: (b, 0, cb))
    return pl.pallas_call(
        _conv_kernel,
        grid=(B, nblk),
        in_specs=[part(0), part(1), part(2), part(3), part(4),
                  pl.BlockSpec((None, SC_WIDTH, CONV_CW), lambda b, cb: (j, 0, cb)),
                  pl.BlockSpec((None, CF_KERNEL, CONV_CW), lambda b, cb: (j, 0, cb)),
                  pl.BlockSpec((None, 1, CONV_CW), lambda b, cb: (j, 0, cb))],
        out_specs=[out_spec, out_spec],
        out_shape=[jax.ShapeDtypeStruct((B, T, HALF), BF16)] * 2,
        scratch_shapes=[pltpu.VMEM((SEQ + 2 * CONV_PAD, CONV_CW), F32)] * 2,
        compiler_params=pltpu.CompilerParams(
            dimension_semantics=("arbitrary", "arbitrary"), vmem_limit_bytes=VMEM_LIMIT),
        name="conv_mixers",
    )(u, u, u, u, u, conv_a, dw_w, dw_b.reshape(dw_b.shape[0], 1, dw_b.shape[1]))


N_RPB_DR = 2 * NA_ROWS - 1
N_RPB_DC = 2 * NA_COLS - 1
ATT_MASKED = N_RPB_DR


def _att_block_geometry(blk):
    r0 = blk * ATT_QROWS
    return min(max(r0 - NA_ROWS // 2, 0), GRID_ROWS - ATT_KROWS)


def _softmax_pv(parts):
    mx = None
    for s, _ in parts:
        pm = jnp.max(s, axis=-1, keepdims=True)
        mx = pm if mx is None else jnp.maximum(mx, pm)
    den = None
    num = None
    for s, v in parts:
        e = jnp.exp(s - mx)
        ps = jnp.sum(e, axis=-1, keepdims=True)
        pv = jnp.dot(e.astype(BF16), v, preferred_element_type=F32)
        den = ps if den is None else den + ps
        num = pv if num is None else num + pv
    return num / den


def _att_kernel(rpb_ref, q_ref, k_ref, v_ref, o_ref, tile_ref, *, rpb_base):
    @pl.when(pl.program_id(1) == 0)
    def _():
        base = (rpb_base + pl.program_id(0)) * (N_RPB_DR * N_RPB_DC)
        ci = lax.broadcasted_iota(I32, (GRID_W, 2 * GRID_W), 0)
        wi = lax.broadcasted_iota(I32, (GRID_W, 2 * GRID_W), 1) & (GRID_W - 1)
        cs = jnp.clip(ci - NA_COLS // 2, 0, GRID_W - NA_COLS)
        dsel = jnp.where((wi >= cs) & (wi < cs + NA_COLS), wi - ci + (NA_COLS - 1), -1)
        masked = jnp.full((GRID_W, 2 * GRID_W), NEG_INF, F32)
        for dr in range(N_RPB_DR):
            t = masked
            for dc in range(N_RPB_DC):
                t = jnp.where(dsel == dc, rpb_ref[base + dr * N_RPB_DC + dc], t)
            tile_ref[dr] = t
        tile_ref[ATT_MASKED] = masked

    def block_bias(blk, kr0):
        low_half = lax.broadcasted_iota(I32, (GRID_W, 2 * GRID_W), 1) < GRID_W
        bands = []
        for qr in range(ATT_QROWS):
            r = blk * ATT_QROWS + qr
            rs = min(max(r - NA_ROWS // 2, 0), GRID_ROWS - NA_ROWS)
            tiles = []
            for jp in range(ATT_KROWS // 2):
                idx = [kr - r + (NA_ROWS - 1) if rs <= kr < rs + NA_ROWS else ATT_MASKED
                       for kr in (kr0 + 2 * jp, kr0 + 2 * jp + 1)]
                if idx[0] == idx[1]:
                    tiles.append(tile_ref[idx[0]])
                else:
                    tiles.append(jnp.where(low_half, tile_ref[idx[0]], tile_ref[idx[1]]))
            bands.append(jnp.concatenate(tiles, axis=1))
        return jnp.concatenate(bands, axis=0)

    scale = NA_DH ** -0.5
    nt = (((1,), (1,)), ((), ()))
    kc = k_ref[pl.ds(0, CTX), :]
    vc = v_ref[pl.ds(0, CTX), :]

    qc = q_ref[pl.ds(0, CTX), :]
    s_cc = lax.dot_general(qc, kc, nt, preferred_element_type=F32) * scale
    o_ref[pl.ds(0, CTX), :] = _softmax_pv([(s_cc, vc)]).astype(BF16)

    for blk in range(GRID_ROWS // ATT_QROWS):
        kr0 = _att_block_geometry(blk)
        qrows = pl.ds(CTX + blk * ATT_Q, ATT_Q)
        krows = pl.ds(CTX + kr0 * GRID_W, ATT_K)
        qb = q_ref[qrows, :]
        s_w = lax.dot_general(qb, k_ref[krows, :], nt, preferred_element_type=F32) * scale
        s_w = s_w + block_bias(blk, kr0)
        s_c = lax.dot_general(qb, kc, nt, preferred_element_type=F32) * scale
        o_ref[qrows, :] = _softmax_pv([(s_w, v_ref[krows, :]), (s_c, vc)]).astype(BF16)


def _attention(u, rpb_flat, j):
    col = lambda off: pl.BlockSpec((None, T, NA_DH), lambda h, b: (b, 0, off + h))
    return pl.pallas_call(
        functools.partial(_att_kernel, rpb_base=j * NA_HEADS),
        grid=(NA_HEADS, B),
        in_specs=[pl.BlockSpec(memory_space=pltpu.SMEM),
                  col(0), col(NA_HEADS), col(2 * NA_HEADS)],
        out_specs=pl.BlockSpec((None, T, NA_DH), lambda h, b: (b, 0, h)),
        out_shape=jax.ShapeDtypeStruct((B, T, HALF), BF16),
        scratch_shapes=[pltpu.VMEM((N_RPB_DR + 1, GRID_W, 2 * GRID_W), F32)],
        compiler_params=pltpu.CompilerParams(
            dimension_semantics=("arbitrary", "arbitrary"), vmem_limit_bytes=VMEM_LIMIT),
        name="neighbourhood_attention",
    )(rpb_flat, u, u, u)


POOL_KWIN = 512


def _pool_kernel(p_ref, pw_ref, ps_ref, o_ref):
    half = lax.shift_left(jnp.int32(1), pl.program_id(1))
    pw = pw_ref[...].astype(BF16)
    sc = ps_ref[...]

    def tile(row0, seq0, length, ks, kwin):
        t = (row0 - seq0) + lax.broadcasted_iota(I32, (ROW_TILE, 1), 0)
        s = ks + lax.broadcasted_iota(I32, (1, kwin), 1)
        lo = t - half
        hi = t + half - 1
        band = jnp.where((s >= lo) & (s <= hi), 1.0, 0.0).astype(BF16)
        ssum = jnp.dot(band, p_ref[pl.ds(seq0 + ks, kwin), :], preferred_element_type=F32)
        cnt = (jnp.minimum(hi, length - 1) - jnp.maximum(lo, 0) + 1).astype(F32)
        mix = ssum / cnt - p_ref[pl.ds(row0, ROW_TILE), :].astype(F32)
        y = jnp.dot(mix.astype(BF16), pw, preferred_element_type=F32) * sc
        o_ref[pl.ds(row0, ROW_TILE), :] = y.astype(BF16)

    tile(0, 0, CTX, 0, CTX)
    for jt in range(SEQ // ROW_TILE):
        ks = min(max(ROW_TILE * jt - (POOL_KWIN - ROW_TILE) // 2, 0), SEQ - POOL_KWIN)
        tile(CTX + ROW_TILE * jt, CTX, SEQ, ks, POOL_KWIN)


def _pool(u, pool_w, pool_scale, j):
    p_block0 = 3 * HALF // POOL_GC
    return pl.pallas_call(
        _pool_kernel,
        grid=(B, N_POOL),
        in_specs=[pl.BlockSpec((None, T, POOL_GC), lambda b, g: (b, 0, p_block0 + g)),
                  pl.BlockSpec((None, None, POOL_GC, POOL_GC), lambda b, g: (j, g, 0, 0)),
                  pl.BlockSpec((None, None, 1, POOL_GC), lambda b, g: (j, g, 0, 0))],
        out_specs=pl.BlockSpec((None, T, POOL_GC), lambda b, g: (b, 0, g)),
        out_shape=jax.ShapeDtypeStruct((B, T, HALF), BF16),
        compiler_params=pltpu.CompilerParams(
            dimension_semantics=("arbitrary", "arbitrary"), vmem_limit_bytes=VMEM_LIMIT),
        name="pool_mixer",
    )(u, pool_w, pool_scale.reshape(pool_scale.shape[0], N_POOL, 1, POOL_GC))


ROUTE_LANES = 128


def _route(logits):
    lane = lax.broadcasted_iota(I32, logits.shape, 1)
    lanef = lane.astype(F32)
    is_g = lane < N_GROUPS
    gl = jnp.where(is_g, logits, NEG_INF)
    ge = jnp.where(is_g, jnp.exp(gl - jnp.max(gl, axis=-1, keepdims=True)), 0.0)
    gp = ge / jnp.sum(ge, axis=-1, keepdims=True)
    g_p = jnp.max(gp, axis=-1, keepdims=True)
    g_idx = jnp.min(jnp.where(is_g & (gp == g_p), lanef, float(ROUTE_LANES)), axis=-1, keepdims=True)
    first = float(N_GROUPS) + float(EPG) * g_idx
    sel = (lanef >= first) & (lanef < first + float(EPG))
    el = jnp.where(sel, logits, NEG_INF)
    ee = jnp.where(sel, jnp.exp(el - jnp.max(el, axis=-1, keepdims=True)), 0.0)
    ep = jnp.where(sel, ee / jnp.sum(ee, axis=-1, keepdims=True), -1.0)
    p1 = jnp.max(ep, axis=-1, keepdims=True)
    i1 = jnp.min(jnp.where(ep == p1, lanef, float(ROUTE_LANES)), axis=-1, keepdims=True)
    ep2 = jnp.where(lanef == i1, -1.0, ep)
    p2 = jnp.max(ep2, axis=-1, keepdims=True)
    i2 = jnp.min(jnp.where(ep2 == p2, lanef, float(ROUTE_LANES)), axis=-1, keepdims=True)
    psum = p1 + p2
    gate1 = g_p * p1 / psum
    gate2 = g_p * p2 / psum
    return jnp.where(lane == 0, i1 - float(N_GROUPS),
                     jnp.where(lane == 1, i2 - float(N_GROUPS),
                               jnp.where(lane == 2, gate1,
                                         jnp.where(lane == 3, gate2, 0.0))))


N_ROW_TILES = NTOK // ROW_TILE
TILES_PER_BATCH = T // ROW_TILE


def _proj_kernel(ya_ref, yb_ref, w_ref, lng_ref, lnb_ref, x_ref, modb_ref, modc_ref, g2_ref,
                 rwh_ref, rwl_ref, rb_ref, xo_ref, hp_ref, ri_ref,
                 lhs_s, out_a, out_b, hh_s, hl_s, vec_s, *, even):
    s = pl.program_id(0)
    pieces = ROW_TILE // NM_CHUNK

    @pl.when(s == 0)
    def _():
        out_b[...] = jnp.zeros_like(out_b)

    is_ctx = jnp.maximum(s - 1, 0) % TILES_PER_BATCH == 0
    mod = jnp.where(is_ctx, modc_ref[...], modb_ref[...])
    vec_s[0:1, :] = mod[2:3, :]
    vec_s[1:2, :] = g2_ref[...] * (1.0 + mod[4:5, :])
    vec_s[2:3, :] = mod[3:4, :]

    def step(mm_out, epi_in):
        for c in range(pieces):
            rows = pl.ds(c * NM_CHUNK, NM_CHUNK)
            xn = x_ref[rows, :] + vec_s[0:1, :] * epi_in[rows, :]
            xo_ref[rows, :] = xn
            inv = lax.rsqrt(jnp.mean(xn * xn, axis=-1, keepdims=True) + RMS_EPS)
            h2 = xn * inv * vec_s[1:2, :] + vec_s[2:3, :]
            hp_ref[rows, :] = h2
            hb = h2.astype(BF16)
            hh_s[rows, :] = hb
            hl_s[rows, :] = (h2 - hb.astype(F32)).astype(BF16)
        hh = hh_s[...]
        logits = (jnp.dot(hh, rwh_ref[...], preferred_element_type=F32)
                  + jnp.dot(hl_s[...], rwh_ref[...], preferred_element_type=F32)
                  + jnp.dot(hh, rwl_ref[...], preferred_element_type=F32)) + rb_ref[...]
        ri_ref[...] = _route(logits)

        if even:
            for c in range(pieces):
                rows = pl.ds(c * NM_CHUNK, NM_CHUNK)
                zc = yb_ref[rows, :].astype(F32)
                dz = zc - jnp.mean(zc, axis=-1, keepdims=True)
                var = jnp.mean(dz * dz, axis=-1, keepdims=True)
                z = dz * lax.rsqrt(var + LN_EPS) * lng_ref[...] + lnb_ref[...]
                lhs_s[rows, pl.ds(HALF, HALF)] = (z * _sigmoid(z)).astype(BF16)
        else:
            lhs_s[:, pl.ds(HALF, HALF)] = yb_ref[...]
        lhs_s[:, pl.ds(0, HALF)] = ya_ref[...]
        for n in range(D // NM_TN):
            cols = pl.ds(n * NM_TN, NM_TN)
            mm_out[:, cols] = jnp.dot(lhs_s[...], w_ref[:, cols], preferred_element_type=F32)

    @pl.when(s % 2 == 0)
    def _():
        step(out_a, out_b)

    @pl.when(s % 2 == 1)
    def _():
        step(out_b, out_a)


def _proj(ya, yb, w_bf, widx, ln_g, ln_b, xc, mods, layer, g2, rw_hi, rw_lo, rb, even):
    def tile_of(t):
        return t // TILES_PER_BATCH, t % TILES_PER_BATCH

    def ahead(width):
        return pl.BlockSpec((None, ROW_TILE, width),
                            lambda s: (*tile_of(jnp.minimum(s, N_ROW_TILES - 1)), 0))

    def behind(width):
        return pl.BlockSpec((None, ROW_TILE, width), lambda s: (*tile_of(jnp.maximum(s - 1, 0)), 0))

    const = lambda *shape: pl.BlockSpec((None,) + shape, lambda s: (widx,) + (0,) * len(shape))
    per_layer = lambda *shape: pl.BlockSpec((None,) + shape, lambda s: (layer,) + (0,) * len(shape))
    return pl.pallas_call(
        functools.partial(_proj_kernel, even=even),
        grid=(N_ROW_TILES + 1,),
        in_specs=[ahead(HALF), ahead(HALF),
                  const(D, D), const(1, HALF), const(1, HALF),
                  behind(D),
                  pl.BlockSpec((None, None, 6, D),
                               lambda s: (layer, jnp.maximum(s - 1, 0) // TILES_PER_BATCH, 0, 0)),
                  pl.BlockSpec((None, None, 6, D), lambda s: (layer, B, 0, 0)),
                  per_layer(1, D), per_layer(D, ROUTE_LANES), per_layer(D, ROUTE_LANES),
                  per_layer(1, ROUTE_LANES)],
        out_specs=[behind(D), behind(D), behind(ROUTE_LANES)],
        out_shape=[jax.ShapeDtypeStruct((B, T, D), F32),
                   jax.ShapeDtypeStruct((B, T, D), F32),
                   jax.ShapeDtypeStruct((B, T, ROUTE_LANES), F32)],
        scratch_shapes=[pltpu.VMEM((ROW_TILE, D), BF16),
                        pltpu.VMEM((ROW_TILE, D), F32),
                        pltpu.VMEM((ROW_TILE, D), F32),
                        pltpu.VMEM((ROW_TILE, D), BF16),
                        pltpu.VMEM((ROW_TILE, D), BF16),
                        pltpu.VMEM((8, D), F32)],
        compiler_params=pltpu.CompilerParams(
            dimension_semantics=("arbitrary",), vmem_limit_bytes=VMEM_LIMIT),
        name="out_proj_router",
    )(ya, yb, w_bf, ln_g, ln_b, xc, mods, mods, g2, rw_hi, rw_lo, rb)


def _positions_kernel(ri_ref, pos_ref, base_ref):
    ntiles = NTOK // ROW_TILE
    lanef = lax.broadcasted_iota(I32, (ROW_TILE, ROUTE_LANES), 1).astype(F32)

    def count(i, acc):
        r = ri_ref[pl.ds(pl.multiple_of(i * ROW_TILE, ROW_TILE), ROW_TILE), :]
        hit = (lanef == r[:, 0:1]) | (lanef == r[:, 1:2])
        return acc + jnp.sum(jnp.where(hit, 1.0, 0.0), axis=0, keepdims=True)

    cnt = lax.fori_loop(0, ntiles, count, jnp.zeros((1, ROUTE_LANES), F32))
    ri = lax.broadcasted_iota(I32, (ROUTE_LANES, ROUTE_LANES), 0)
    ci = lax.broadcasted_iota(I32, (ROUTE_LANES, ROUTE_LANES), 1)
    before = jnp.where(ri < ci, 1.0, 0.0)
    cnt8 = jnp.broadcast_to(cnt, (8, ROUTE_LANES))
    base8 = jnp.dot(cnt8, before, preferred_element_type=F32, precision=HIGHEST)
    base_ref[...] = base8.astype(I32)
    base_col = jnp.broadcast_to(base8[0:1, :], (ROUTE_LANES, ROUTE_LANES)).T[:, 0:1]

    ti = lax.broadcasted_iota(I32, (ROW_TILE, ROW_TILE), 0)
    tj = lax.broadcasted_iota(I32, (ROW_TILE, ROW_TILE), 1)
    earlier = jnp.where(ti < tj, 1.0, 0.0).astype(BF16)
    subf = lax.broadcasted_iota(I32, (ROUTE_LANES, ROW_TILE), 0).astype(F32)

    def place(i, carry):
        start = pl.multiple_of(i * ROW_TILE, ROW_TILE)
        rt = ri_ref[pl.ds(start, ROW_TILE), :].T
        e0 = rt[0:1, :]
        e1 = rt[1:2, :]
        hit0 = subf == e0
        hit1 = subf == e1
        hits = jnp.where(hit0 | hit1, 1.0, 0.0)
        rank = jnp.dot(hits.astype(BF16), earlier, preferred_element_type=F32) + carry + base_col
        pos_ref[0:1, pl.ds(start, ROW_TILE)] = jnp.sum(
            jnp.where(hit0, rank, 0.0), axis=0, keepdims=True).astype(I32)
        pos_ref[1:2, pl.ds(start, ROW_TILE)] = jnp.sum(
            jnp.where(hit1, rank, 0.0), axis=0, keepdims=True).astype(I32)
        return carry + jnp.sum(hits, axis=1, keepdims=True)

    lax.fori_loop(0, ntiles, place, jnp.zeros((ROUTE_LANES, 1), F32))


def _positions(ri_flat):
    return pl.pallas_call(
        _positions_kernel,
        out_shape=[jax.ShapeDtypeStruct((2, NTOK), I32),
                   jax.ShapeDtypeStruct((8, ROUTE_LANES), I32)],
        compiler_params=pltpu.CompilerParams(vmem_limit_bytes=VMEM_LIMIT),
        name="moe_positions",
    )(ri_flat)


INVERT_UNROLL = 8


def _items_kernel(base_ref, pos_ref, it_ref, src_ref):
    def init(i, carry):
        it_ref[ITEM_NEW + i] = 0
        it_ref[ITEM_SLOT + i] = 0
        it_ref[ITEM_NEXT + i] = -1
        return carry

    lax.fori_loop(0, N_ITEMS, init, 0)

    def per_expert(e, carry):
        n, run, run_start = carry
        s = base_ref[e]
        t = base_ref[e + 1]
        first = lax.shift_right_logical(s, 8)
        last = lax.shift_right_logical(jnp.maximum(t - 1, 0), 8)
        used = t > s
        ntile = jnp.where(used, last - first + 1, 0)

        def per_tile(k, n2):
            tile = first + k
            it_ref[n2] = tile
            it_ref[N_ITEMS + n2] = e
            it_ref[2 * N_ITEMS + n2] = jnp.maximum(s - tile * ROW_TILE, 0)
            it_ref[3 * N_ITEMS + n2] = jnp.minimum(t - tile * ROW_TILE, ROW_TILE)
            return n2 + 1

        n_end = lax.fori_loop(0, ntile, per_tile, n)
        it_ref[ITEM_NEW + n] = jnp.where(used, 1, it_ref[ITEM_NEW + n])
        it_ref[ITEM_SLOT + n] = jnp.where(used, run & 1, it_ref[ITEM_SLOT + n])
        link = used & (run > 0)
        it_ref[ITEM_NEXT + run_start] = jnp.where(link, e, it_ref[ITEM_NEXT + run_start])
        return n_end, run + jnp.where(used, 1, 0), jnp.where(used, n, run_start)

    n_used, _, _ = lax.fori_loop(0, N_EXPERTS, per_expert,
                                 (jnp.int32(0), jnp.int32(0), jnp.int32(0)))
    last_e = it_ref[N_ITEMS + n_used - 1]

    def pad(i, carry):
        it_ref[i] = N_SORT_TILES - 1
        it_ref[N_ITEMS + i] = last_e
        it_ref[2 * N_ITEMS + i] = 0
        it_ref[3 * N_ITEMS + i] = 0
        return carry

    lax.fori_loop(n_used, N_ITEMS, pad, 0)

    def invert(g, carry):
        for k in range(INVERT_UNROLL):
            t = g * INVERT_UNROLL + k
            src_ref[pos_ref[t]] = t
            src_ref[pos_ref[NTOK + t]] = t
        return carry

    lax.fori_loop(0, NTOK // INVERT_UNROLL, invert, 0)


def _items(base_flat, pos_flat):
    smem = pl.BlockSpec(memory_space=pltpu.SMEM)
    return pl.pallas_call(
        _items_kernel,
        in_specs=[smem, smem],
        out_specs=[smem, smem],
        out_shape=[jax.ShapeDtypeStruct((7 * N_ITEMS,), I32),
                   jax.ShapeDtypeStruct((NSLOT,), I32)],
        name="moe_items",
    )(base_flat, pos_flat)


ROW_DMA_UNROLL = 8


def _gmm_kernel(it_ref, src_ref, h_hbm, w1_hbm, w3_hbm, w2_hbm, o_ref,
                xbuf_a, xbuf_b, w1_raw, w3_raw, w2_raw, w1_bf, w3_bf, w2_bf, sems, row_sems,
                *, layer):
    i = pl.program_id(0)
    tile = it_ref[i]
    expert = it_ref[N_ITEMS + i]
    lo = it_ref[2 * N_ITEMS + i]
    hi = it_ref[3 * N_ITEMS + i]
    slot = it_ref[ITEM_SLOT + i]
    nxt = it_ref[ITEM_NEXT + i]
    prev_tile = it_ref[jnp.maximum(i - 1, 0)]
    xbufs = (xbuf_a, xbuf_b)

    def gather_rows(tl, p, unroll):
        def rows(g, carry):
            for k in range(unroll):
                r = g * unroll + k
                pltpu.make_async_copy(h_hbm.at[pl.ds(src_ref[tl * ROW_TILE + r], 1)],
                                      xbufs[p].at[pl.ds(r, 1)], row_sems.at[p]).start()
            return carry

        if unroll == ROW_TILE:
            rows(0, 0)
        else:
            lax.fori_loop(0, ROW_TILE // unroll, rows, 0)

    def weight_copies(e, s):
        return (pltpu.make_async_copy(w1_hbm.at[layer, e], w1_raw.at[s], sems.at[s, 0]),
                pltpu.make_async_copy(w3_hbm.at[layer, e], w3_raw.at[s], sems.at[s, 1]),
                pltpu.make_async_copy(w2_hbm.at[layer, e], w2_raw.at[s], sems.at[s, 2]))

    @pl.when(it_ref[ITEM_NEW + i] == 1)
    def _():
        @pl.when(i == 0)
        def _():
            for cp in weight_copies(expert, slot):
                cp.start()

        for cp in weight_copies(expert, slot):
            cp.wait()

        @pl.when(nxt >= 0)
        def _():
            for cp in weight_copies(nxt, 1 - slot):
                cp.start()

        w1_bf[...] = w1_raw[slot].astype(BF16)
        w3_bf[...] = w3_raw[slot].astype(BF16)
        w2_bf[...] = w2_raw[slot].astype(BF16)

    new_tile = (i == 0) | (tile != prev_tile)
    prefetch = new_tile & (tile + 1 < N_SORT_TILES)

    def tile_work(p):
        cur = xbufs[p]

        @pl.when(new_tile)
        def _():
            @pl.when(i == 0)
            def _():
                gather_rows(tile, p, ROW_DMA_UNROLL)

            pltpu.make_async_copy(cur, cur, row_sems.at[p]).wait()
            o_ref[...] = jnp.zeros_like(o_ref)

        def experts():
            xb = cur[...].astype(BF16)
            a = jnp.dot(xb, w1_bf[...], preferred_element_type=F32)
            b = jnp.dot(xb, w3_bf[...], preferred_element_type=F32)
            rows = lax.broadcasted_iota(I32, (ROW_TILE, 1), 0)
            hid = jnp.where((rows >= lo) & (rows < hi), a * _sigmoid(a) * b, 0.0)
            o_ref[...] += jnp.dot(hid.astype(BF16), w2_bf[...], preferred_element_type=F32)

        @pl.when(prefetch)
        def _():
            gather_rows(tile + 1, 1 - p, ROW_TILE)
            experts()

        @pl.when(jnp.logical_not(prefetch) & (hi > lo))
        def _():
            experts()

    for p in range(2):
        pl.when(tile & 1 == p)(functools.partial(tile_work, p))


def _gmm(items, src, h_flat, w1, w3, w2, layer):
    hbm = pl.BlockSpec(memory_space=pl.ANY)
    return pl.pallas_call(
        functools.partial(_gmm_kernel, layer=layer),
        grid_spec=pltpu.PrefetchScalarGridSpec(
            num_scalar_prefetch=2,
            grid=(N_ITEMS,),
            in_specs=[hbm, hbm, hbm, hbm],
            out_specs=pl.BlockSpec((ROW_TILE, D), lambda i, it, src: (it[i], 0)),
            scratch_shapes=[pltpu.VMEM((ROW_TILE, D), F32),
                            pltpu.VMEM((ROW_TILE, D), F32),
                            pltpu.VMEM((2, D, EXPERT_FF), F32),
                            pltpu.VMEM((2, D, EXPERT_FF), F32),
                            pltpu.VMEM((2, EXPERT_FF, D), F32),
                            pltpu.VMEM((D, EXPERT_FF), BF16),
                            pltpu.VMEM((D, EXPERT_FF), BF16),
                            pltpu.VMEM((EXPERT_FF, D), BF16),
                            pltpu.SemaphoreType.DMA((2, 3)),
                            pltpu.SemaphoreType.DMA((2,))],
        ),
        out_shape=jax.ShapeDtypeStruct((NSLOT, D), F32),
        compiler_params=pltpu.CompilerParams(
            dimension_semantics=("arbitrary",), vmem_limit_bytes=VMEM_LIMIT),
        name="moe_experts",
    )(items, src, h_flat, w1, w3, w2)


def _combine_kernel(pos_ref, ys_hbm, x_ref, ri_ref, modb_ref, modc_ref, fg_ref, o_ref,
                    buf_a, buf_b, gate_s, sems, *, final, tiles_per_b, m_off):
    i = pl.program_id(0)
    ntiles = pl.num_programs(0)
    bufs = (buf_a, buf_b)

    def issue(tile_i, p, unroll):
        t0 = (tile_i // tiles_per_b) * T + (tile_i % tiles_per_b + m_off) * ROW_TILE

        def rows(g, carry):
            for k in range(unroll):
                r = g * unroll + k
                for s in range(2):
                    pltpu.make_async_copy(ys_hbm.at[pl.ds(pos_ref[s * NTOK + t0 + r], 1)],
                                          bufs[p].at[s, pl.ds(r, 1)], sems.at[p]).start()
            return carry

        if unroll == ROW_TILE:
            rows(0, 0)
        else:
            lax.fori_loop(0, ROW_TILE // unroll, rows, 0)

    @pl.when(i == 0)
    def _():
        issue(0, 0, ROW_DMA_UNROLL)

    m = i % tiles_per_b + m_off
    gate_s[...] = jnp.where(m == 0, modc_ref[5:6, :], modb_ref[5:6, :])

    def tile_work(p):
        cur = bufs[p]
        pltpu.make_async_copy(cur, cur, sems.at[p]).wait()

        def gated_residual():
            for c in range(ROW_TILE // NM_CHUNK):
                rows = pl.ds(c * NM_CHUNK, NM_CHUNK)
                g0 = ri_ref[rows, 2:3]
                g1 = ri_ref[rows, 3:4]
                xn = x_ref[rows, :] + gate_s[...] * (g0 * cur[0, rows, :] + g1 * cur[1, rows, :])
                if final:
                    inv = lax.rsqrt(jnp.mean(xn * xn, axis=-1, keepdims=True) + RMS_EPS)
                    xn = xn * inv * fg_ref[...]
                o_ref[rows, :] = xn

        @pl.when(i + 1 < ntiles)
        def _():
            issue(i + 1, 1 - p, ROW_TILE)
            gated_residual()

        @pl.when(i + 1 >= ntiles)
        def _():
            gated_residual()

    for p in range(2):
        pl.when(i % 2 == p)(functools.partial(tile_work, p))


def _combine(pos_flat, ys, xc, ri, mods, layer, final_g, final):
    if final:
        tiles_per_b, m_off, rows_out = SEQ // ROW_TILE, CTX // ROW_TILE, SEQ
    else:
        tiles_per_b, m_off, rows_out = T // ROW_TILE, 0, T
    tok = lambda width: pl.BlockSpec(
        (None, ROW_TILE, width), lambda i, pos: (i // tiles_per_b, i % tiles_per_b + m_off, 0))
    return pl.pallas_call(
        functools.partial(_combine_kernel, final=final, tiles_per_b=tiles_per_b, m_off=m_off),
        grid_spec=pltpu.PrefetchScalarGridSpec(
            num_scalar_prefetch=1,
            grid=(B * tiles_per_b,),
            in_specs=[
                pl.BlockSpec(memory_space=pl.ANY),
                tok(D),
                tok(ROUTE_LANES),
                pl.BlockSpec((None, None, 6, D), lambda i, pos: (layer, i // tiles_per_b, 0, 0)),
                pl.BlockSpec((None, None, 6, D), lambda i, pos: (layer, B, 0, 0)),
                pl.BlockSpec((1, D), lambda i, pos: (0, 0)),
            ],
            out_specs=pl.BlockSpec((None, ROW_TILE, D),
                                   lambda i, pos: (i // tiles_per_b, i % tiles_per_b, 0)),
            scratch_shapes=[pltpu.VMEM((2, ROW_TILE, D), F32),
                            pltpu.VMEM((2, ROW_TILE, D), F32),
                            pltpu.VMEM((1, D), F32),
                            pltpu.SemaphoreType.DMA((2,))],
        ),
        out_shape=jax.ShapeDtypeStruct((B, rows_out, D), F32),
        compiler_params=pltpu.CompilerParams(
            dimension_semantics=("arbitrary",), vmem_limit_bytes=VMEM_LIMIT),
        name="moe_combine",
    )(pos_flat, ys, xc, ri, mods, mods, final_g)


def kernel(x, c, ctx, c_ctx, ada_w, ada_b, norm1_g, norm2_g, ev_w_in, ev_conv_a, ev_dw_w, ev_dw_b,
           ev_ln_g, ev_ln_b, ev_w_out, od_w_in, od_rpb, od_pool_w, od_pool_scale, od_w_out,
           moe_rg_w, moe_rg_b, moe_re_w, moe_re_b, moe_w1, moe_w3, moe_w2, final_g):
    xc = jnp.concatenate([ctx, x], axis=1)
    c8 = jnp.concatenate([c, c_ctx[None, :], jnp.zeros((8 - B - 1, D), F32)], axis=0)
    mods = _ada(c8, ada_w, ada_b).reshape(DEPTH, 8, 6, D)

    ev_w_in_bf = ev_w_in.astype(BF16)
    od_w_in_bf = od_w_in.astype(BF16)
    ev_w_out_bf = ev_w_out.astype(BF16)
    od_w_out_bf = od_w_out.astype(BF16)
    pad = ROUTE_LANES - N_GROUPS - N_EXPERTS
    rw = jnp.concatenate([moe_rg_w, moe_re_w, jnp.zeros((DEPTH, D, pad), F32)], axis=-1)
    rw_hi = rw.astype(BF16)
    rw_lo = (rw - rw_hi.astype(F32)).astype(BF16)
    rb = jnp.concatenate([moe_rg_b, moe_re_b, jnp.zeros((DEPTH, pad), F32)],
                         axis=-1).reshape(DEPTH, 1, ROUTE_LANES)
    final_g2 = final_g.reshape(1, D)
    norm1_g = norm1_g.reshape(DEPTH, 1, D)
    norm2_g = norm2_g.reshape(DEPTH, 1, D)
    ev_ln_g = ev_ln_g.reshape(-1, 1, HALF)
    ev_ln_b = ev_ln_b.reshape(-1, 1, HALF)

    for i in range(DEPTH):
        j = i // 2
        last = i == DEPTH - 1
        if i % 2 == 1:
            u = _norm_mm(xc, mods, i, norm1_g, od_w_in_bf, j)
            ya = _attention(u, od_rpb.reshape(-1), j)
            yb = _pool(u, od_pool_w, od_pool_scale, j)
            xc, hp, ri = _proj(ya, yb, od_w_out_bf, j, ev_ln_g, ev_ln_b, xc, mods, i, norm2_g,
                               rw_hi, rw_lo, rb, even=False)
        else:
            u = _norm_mm(xc, mods, i, norm1_g, ev_w_in_bf, j)
            ya, yb = _conv(u, ev_conv_a, ev_dw_w, ev_dw_b, j)
            xc, hp, ri = _proj(ya, yb, ev_w_out_bf, j, ev_ln_g, ev_ln_b, xc, mods, i, norm2_g,
                               rw_hi, rw_lo, rb, even=True)
        pos, base = _positions(ri.reshape(NTOK, ROUTE_LANES))
        pos_flat = pos.reshape(2 * NTOK)
        items, src = _items(base[0, :N_EXPERTS + 1 + 7], pos_flat)
        ys = _gmm(items, src, hp.reshape(NTOK, D), moe_w1, moe_w3, moe_w2, i)
        xc = _combine(pos_flat, ys, xc, ri, mods, i, final_g2, last)
    return xc
```

```python
import functools

import numpy as np
import jax
import jax.numpy as jnp
from jax import lax
from jax.experimental import pallas as pl
from jax.experimental.pallas import tpu as pltpu

F32 = jnp.float32
BF16 = jnp.bfloat16
I32 = jnp.int32
HIGHEST = lax.Precision.HIGHEST

D = 2048
B = 4
SEQ = 2048
CTX = 256
T = CTX + SEQ
NTOK = B * T
DEPTH = 4
GRID_W = 64
GRID_ROWS = SEQ // GRID_W
HALF = D // 2
SC_W = 1024
CF_KERNEL = 31
SC_WIDTH = 3
NA_HEADS = 8
NA_DH = 128
NA_ROWS = 8
NA_COLS = 16
N_POOL = 4
POOL_GC = 256
N_EXPERTS = 32
N_GROUPS = 4
EPG = 8
EXPERT_FF = 256
RMS_EPS = 1e-6
LN_EPS = 1e-5
NEG_INF = -1e30

ROW_TILE = 256
NSLOT = 2 * NTOK
N_SORT_TILES = NSLOT // ROW_TILE
N_ITEMS = 104
ITEM_NEW, ITEM_SLOT, ITEM_NEXT = 4 * N_ITEMS, 5 * N_ITEMS, 6 * N_ITEMS
NM_TN = 1024
NM_CHUNK = 16
ATT_QROWS = 4
ATT_Q = ATT_QROWS * GRID_W
ATT_KROWS = 12
ATT_K = ATT_KROWS * GRID_W
CONV_CW = 256
CONV_PAD = 16
CONV_ROWS = 32
VMEM_LIMIT = 56 * 1024 * 1024


def _sigmoid(v):
    return 1.0 / (1.0 + jnp.exp(-v))


ADA_TN = 1024


def _ada_kernel(c_ref, w_ref, b_ref, o_ref):
    cv = c_ref[...]
    s = cv * _sigmoid(cv)
    s_hi = s.astype(BF16)
    s_lo = (s - s_hi.astype(F32)).astype(BF16)
    w = w_ref[0]
    w_hi = w.astype(BF16)
    w_lo = (w - w_hi.astype(F32)).astype(BF16)
    o_ref[0] = (jnp.dot(s_hi, w_hi, preferred_element_type=F32)
                + jnp.dot(s_lo, w_hi, preferred_element_type=F32)
                + jnp.dot(s_hi, w_lo, preferred_element_type=F32)) + b_ref[0]


def _ada(c8, ada_w, ada_b):
    n = ada_w.shape[-1]
    return pl.pallas_call(
        _ada_kernel,
        grid=(DEPTH, n // ADA_TN),
        in_specs=[
            pl.BlockSpec((8, D), lambda l, j: (0, 0)),
            pl.BlockSpec((1, D, ADA_TN), lambda l, j: (l, 0, j)),
            pl.BlockSpec((1, 1, ADA_TN), lambda l, j: (l, 0, j)),
        ],
        out_specs=pl.BlockSpec((1, 8, ADA_TN), lambda l, j: (l, 0, j)),
        out_shape=jax.ShapeDtypeStruct((DEPTH, 8, n), F32),
        compiler_params=pltpu.CompilerParams(
            dimension_semantics=("arbitrary", "arbitrary"), vmem_limit_bytes=VMEM_LIMIT),
        name="ada",
    )(c8, ada_w, ada_b.reshape(DEPTH, 1, n))


def _norm_mm_kernel(x_ref, modb_ref, modc_ref, g_ref, w_ref, o_ref, h_ref, gain_ref, shift_ref):
    is_ctx = pl.program_id(1) == 0
    gain_ref[...] = g_ref[...] * (1.0 + jnp.where(is_ctx, modc_ref[1:2, :], modb_ref[1:2, :]))
    shift_ref[...] = jnp.where(is_ctx, modc_ref[0:1, :], modb_ref[0:1, :])
    for c in range(ROW_TILE // NM_CHUNK):
        rows = pl.ds(c * NM_CHUNK, NM_CHUNK)
        xv = x_ref[rows, :]
        inv = lax.rsqrt(jnp.mean(xv * xv, axis=-1, keepdims=True) + RMS_EPS)
        h_ref[rows, :] = (xv * inv * gain_ref[...] + shift_ref[...]).astype(BF16)
    for n in range(o_ref.shape[1] // NM_TN):
        cols = pl.ds(n * NM_TN, NM_TN)
        o_ref[:, cols] = jnp.dot(h_ref[...], w_ref[:, cols],
                                 preferred_element_type=F32).astype(BF16)


def _norm_mm(xc, mods, layer, g, w_bf, widx):
    n_out = w_bf.shape[-1]
    return pl.pallas_call(
        _norm_mm_kernel,
        grid=(B, T // ROW_TILE),
        in_specs=[
            pl.BlockSpec((None, ROW_TILE, D), lambda b, m: (b, m, 0)),
            pl.BlockSpec((None, None, 6, D), lambda b, m: (layer, b, 0, 0)),
            pl.BlockSpec((None, None, 6, D), lambda b, m: (layer, B, 0, 0)),
            pl.BlockSpec((None, 1, D), lambda b, m: (layer, 0, 0)),
            pl.BlockSpec((None, D, n_out), lambda b, m: (widx, 0, 0), pipeline_mode=pl.Buffered(1)),
        ],
        out_specs=pl.BlockSpec((None, ROW_TILE, n_out), lambda b, m: (b, m, 0)),
        out_shape=jax.ShapeDtypeStruct((B, T, n_out), BF16),
        scratch_shapes=[pltpu.VMEM((ROW_TILE, D), BF16),
                        pltpu.VMEM((1, D), F32), pltpu.VMEM((1, D), F32)],
        compiler_params=pltpu.CompilerParams(
            dimension_semantics=("arbitrary", "arbitrary"), vmem_limit_bytes=VMEM_LIMIT),
        name="norm_mm",
    )(xc, mods, mods, g, w_bf)


def _depthwise_taps(win, taps, width):
    nwin = CONV_ROWS + 2 * CONV_PAD
    acc = jnp.zeros((CONV_ROWS, win.shape[1]), F32)
    for res in range(8):
        offsets = [o for o in range(res, nwin - CONV_ROWS + 1, 8)
                   if 0 <= o - CONV_PAD + width // 2 < width]
        if not offsets:
            continue
        rolled = win if res == 0 else pltpu.roll(win, nwin - res, 0)
        for o in offsets:
            k = o - CONV_PAD + width // 2
            acc = acc + rolled[o - res:o - res + CONV_ROWS, :] * taps[k:k + 1, :]
    return acc


def _conv_kernel(bg_ref, cg_ref, xa_ref, ga_ref, gb_ref, ca_ref, dw_ref, dwb_ref,
                 ya_ref, zc_ref, vs_ref, zs_ref):
    ca = ca_ref[...]
    dw = dw_ref[...]
    dwb = dwb_ref[...]
    zeros_pad = jnp.zeros((CONV_PAD, CONV_CW), F32)

    def run_sequence(seq0, length):
        vs_ref[pl.ds(0, CONV_PAD), :] = zeros_pad
        zs_ref[pl.ds(0, CONV_PAD), :] = zeros_pad
        vs_ref[pl.ds(CONV_PAD + length, CONV_PAD), :] = zeros_pad
        zs_ref[pl.ds(CONV_PAD + length, CONV_PAD), :] = zeros_pad

        def fill(i, carry):
            r = pl.multiple_of(i * ROW_TILE, ROW_TILE)
            src = pl.ds(seq0 + r, ROW_TILE)
            dst = pl.ds(CONV_PAD + r, ROW_TILE)
            vs_ref[dst, :] = cg_ref[src, :].astype(F32) * xa_ref[src, :].astype(F32)
            zs_ref[dst, :] = ga_ref[src, :].astype(F32) * _sigmoid(gb_ref[src, :].astype(F32))
            return carry

        lax.fori_loop(0, length // ROW_TILE, fill, 0)

        def conv(i, carry):
            r = pl.multiple_of(i * CONV_ROWS, CONV_ROWS)
            win = pl.ds(r, CONV_ROWS + 2 * CONV_PAD)
            out = pl.ds(seq0 + r, CONV_ROWS)
            acc3 = _depthwise_taps(vs_ref[win, :], ca, SC_WIDTH)
            ya_ref[out, :] = (bg_ref[out, :].astype(F32) * acc3).astype(BF16)
            acc = _depthwise_taps(zs_ref[win, :], dw, CF_KERNEL)
            zc_ref[out, :] = (acc + dwb).astype(BF16)
            return carry

        lax.fori_loop(0, length // CONV_ROWS, conv, 0)

    run_sequence(0, CTX)
    run_sequence(CTX, SEQ)


def _conv(u, conv_a, dw_w, dw_b, j):
    nblk = SC_W // CONV_CW
    part = lambda k: pl.BlockSpec((None, T, CONV_CW), lambda b, cb: (b, 0, k * nblk + cb))
    out_spec = pl.BlockSpec((None, T, CONV_CW), lambda b, cb: (b, 0, cb))
    return pl.pallas_call(
        _conv_kernel,
        grid=(B, nblk),
        in_specs=[part(0), part(1), part(2), part(3), part(4),
                  pl.BlockSpec((None, SC_WIDTH, CONV_CW), lambda b, cb: (j, 0, cb)),
                  pl.BlockSpec((None, CF_KERNEL, CONV_CW), lambda b, cb: (j, 0, cb)),
                  pl.BlockSpec((None, 1, CONV_CW), lambda b, cb: (j, 0, cb))],
        out_specs=[out_spec, out_spec],
        out_shape=[jax.ShapeDtypeStruct((B, T, HALF), BF16)] * 2,
        scratch_shapes=[pltpu.VMEM((SEQ + 2 * CONV_PAD, CONV_CW), F32)] * 2,
        compiler_params=pltpu.CompilerParams(
            dimension_semantics=("arbitrary", "arbitrary"), vmem_limit_bytes=VMEM_LIMIT),
        name="conv_mixers",
    )(u, u, u, u, u, conv_a, dw_w, dw_b.reshape(dw_b.shape[0], 1, dw_b.shape[1]))


N_RPB_DR = 2 * NA_ROWS - 1
N_RPB_DC = 2 * NA_COLS - 1
ATT_MASKED = N_RPB_DR


def _att_block_geometry(blk):
    r0 = blk * ATT_QROWS
    return min(max(r0 - NA_ROWS // 2, 0), GRID_ROWS - ATT_KROWS)


def _softmax_pv(parts):
    mx = None
    for s, _ in parts:
        pm = jnp.max(s, axis=-1, keepdims=True)
        mx = pm if mx is None else jnp.maximum(mx, pm)
    den = None
    num = None
    for s, v in parts:
        e = jnp.exp(s - mx)
        ps = jnp.sum(e, axis=-1, keepdims=True)
        pv = jnp.dot(e.astype(BF16), v, preferred_element_type=F32)
        den = ps if den is None else den + ps
        num = pv if num is None else num + pv
    return num / den


def _att_kernel(rpb_ref, q_ref, k_ref, v_ref, o_ref, tile_ref, *, rpb_base):
    @pl.when(pl.program_id(1) == 0)
    def _():
        base = (rpb_base + pl.program_id(0)) * (N_RPB_DR * N_RPB_DC)
        ci = lax.broadcasted_iota(I32, (GRID_W, 2 * GRID_W), 0)
        wi = lax.broadcasted_iota(I32, (GRID_W, 2 * GRID_W), 1) & (GRID_W - 1)
        cs = jnp.clip(ci - NA_COLS // 2, 0, GRID_W - NA_COLS)
        dsel = jnp.where((wi >= cs) & (wi < cs + NA_COLS), wi - ci + (NA_COLS - 1), -1)
        masked = jnp.full((GRID_W, 2 * GRID_W), NEG_INF, F32)
        for dr in range(N_RPB_DR):
            t = masked
            for dc in range(N_RPB_DC):
                t = jnp.where(dsel == dc, rpb_ref[base + dr * N_RPB_DC + dc], t)
            tile_ref[dr] = t
        tile_ref[ATT_MASKED] = masked

    def block_bias(blk, kr0):
        low_half = lax.broadcasted_iota(I32, (GRID_W, 2 * GRID_W), 1) < GRID_W
        bands = []
        for qr in range(ATT_QROWS):
            r = blk * ATT_QROWS + qr
            rs = min(max(r - NA_ROWS // 2, 0), GRID_ROWS - NA_ROWS)
            tiles = []
            for jp in range(ATT_KROWS // 2):
                idx = [kr - r + (NA_ROWS - 1) if rs <= kr < rs + NA_ROWS else ATT_MASKED
                       for kr in (kr0 + 2 * jp, kr0 + 2 * jp + 1)]
                if idx[0] == idx[1]:
                    tiles.append(tile_ref[idx[0]])
                else:
                    tiles.append(jnp.where(low_half, tile_ref[idx[0]], tile_ref[idx[1]]))
            bands.append(jnp.concatenate(tiles, axis=1))
        return jnp.concatenate(bands, axis=0)

    scale = NA_DH ** -0.5
    nt = (((1,), (1,)), ((), ()))
    kc = k_ref[pl.ds(0, CTX), :]
    vc = v_ref[pl.ds(0, CTX), :]

    qc = q_ref[pl.ds(0, CTX), :]
    s_cc = lax.dot_general(qc, kc, nt, preferred_element_type=F32) * scale
    o_ref[pl.ds(0, CTX), :] = _softmax_pv([(s_cc, vc)]).astype(BF16)

    for blk in range(GRID_ROWS // ATT_QROWS):
        kr0 = _att_block_geometry(blk)
        qrows = pl.ds(CTX + blk * ATT_Q, ATT_Q)
        krows = pl.ds(CTX + kr0 * GRID_W, ATT_K)
        qb = q_ref[qrows, :]
        s_w = lax.dot_general(qb, k_ref[krows, :], nt, preferred_element_type=F32) * scale
        s_w = s_w + block_bias(blk, kr0)
        s_c = lax.dot_general(qb, kc, nt, preferred_element_type=F32) * scale
        o_ref[qrows, :] = _softmax_pv([(s_w, v_ref[krows, :]), (s_c, vc)]).astype(BF16)


def _attention(u, rpb_flat, j):
    col = lambda off: pl.BlockSpec((None, T, NA_DH), lambda h, b: (b, 0, off + h))
    return pl.pallas_call(
        functools.partial(_att_kernel, rpb_base=j * NA_HEADS),
        grid=(NA_HEADS, B),
        in_specs=[pl.BlockSpec(memory_space=pltpu.SMEM),
                  col(0), col(NA_HEADS), col(2 * NA_HEADS)],
        out_specs=pl.BlockSpec((None, T, NA_DH), lambda h, b: (b, 0, h)),
        out_shape=jax.ShapeDtypeStruct((B, T, HALF), BF16),
        scratch_shapes=[pltpu.VMEM((N_RPB_DR + 1, GRID_W, 2 * GRID_W), F32)],
        compiler_params=pltpu.CompilerParams(
            dimension_semantics=("arbitrary", "arbitrary"), vmem_limit_bytes=VMEM_LIMIT),
        name="neighbourhood_attention",
    )(rpb_flat, u, u, u)


POOL_KWIN = 512


def _pool_kernel(p_ref, pw_ref, ps_ref, o_ref):
    half = lax.shift_left(jnp.int32(1), pl.program_id(1))
    pw = pw_ref[...].astype(BF16)
    sc = ps_ref[...]

    def tile(row0, seq0, length, ks, kwin):
        t = (row0 - seq0) + lax.broadcasted_iota(I32, (ROW_TILE, 1), 0)
        s = ks + lax.broadcasted_iota(I32, (1, kwin), 1)
        lo = t - half
        hi = t + half - 1
        band = jnp.where((s >= lo) & (s <= hi), 1.0, 0.0).astype(BF16)
        ssum = jnp.dot(band, p_ref[pl.ds(seq0 + ks, kwin), :], preferred_element_type=F32)
        cnt = (jnp.minimum(hi, length - 1) - jnp.maximum(lo, 0) + 1).astype(F32)
        mix = ssum / cnt - p_ref[pl.ds(row0, ROW_TILE), :].astype(F32)
        y = jnp.dot(mix.astype(BF16), pw, preferred_element_type=F32) * sc
        o_ref[pl.ds(row0, ROW_TILE), :] = y.astype(BF16)

    tile(0, 0, CTX, 0, CTX)
    for jt in range(SEQ // ROW_TILE):
        ks = min(max(ROW_TILE * jt - (POOL_KWIN - ROW_TILE) // 2, 0), SEQ - POOL_KWIN)
        tile(CTX + ROW_TILE * jt, CTX, SEQ, ks, POOL_KWIN)


def _pool(u, pool_w, pool_scale, j):
    p_block0 = 3 * HALF // POOL_GC
    return pl.pallas_call(
        _pool_kernel,
        grid=(B, N_POOL),
        in_specs=[pl.BlockSpec((None, T, POOL_GC), lambda b, g: (b, 0, p_block0 + g)),
                  pl.BlockSpec((None, None, POOL_GC, POOL_GC), lambda b, g: (j, g, 0, 0)),
                  pl.BlockSpec((None, None, 1, POOL_GC), lambda b, g: (j, g, 0, 0))],
        out_specs=pl.BlockSpec((None, T, POOL_GC), lambda b, g: (b, 0, g)),
        out_shape=jax.ShapeDtypeStruct((B, T, HALF), BF16),
        compiler_params=pltpu.CompilerParams(
            dimension_semantics=("arbitrary", "arbitrary"), vmem_limit_bytes=VMEM_LIMIT),
        name="pool_mixer",
    )(u, pool_w, pool_scale.reshape(pool_scale.shape[0], N_POOL, 1, POOL_GC))


ROUTE_LANES = 128


def _route(logits):
    lane = lax.broadcasted_iota(I32, logits.shape, 1)
    lanef = lane.astype(F32)
    is_g = lane < N_GROUPS
    gl = jnp.where(is_g, logits, NEG_INF)
    ge = jnp.where(is_g, jnp.exp(gl - jnp.max(gl, axis=-1, keepdims=True)), 0.0)
    gp = ge / jnp.sum(ge, axis=-1, keepdims=True)
    g_p = jnp.max(gp, axis=-1, keepdims=True)
    g_idx = jnp.min(jnp.where(is_g & (gp == g_p), lanef, float(ROUTE_LANES)), axis=-1, keepdims=True)
    first = float(N_GROUPS) + float(EPG) * g_idx
    sel = (lanef >= first) & (lanef < first + float(EPG))
    el = jnp.where(sel, logits, NEG_INF)
    ee = jnp.where(sel, jnp.exp(el - jnp.max(el, axis=-1, keepdims=True)), 0.0)
    ep = jnp.where(sel, ee / jnp.sum(ee, axis=-1, keepdims=True), -1.0)
    p1 = jnp.max(ep, axis=-1, keepdims=True)
    i1 = jnp.min(jnp.where(ep == p1, lanef, float(ROUTE_LANES)), axis=-1, keepdims=True)
    ep2 = jnp.where(lanef == i1, -1.0, ep)
    p2 = jnp.max(ep2, axis=-1, keepdims=True)
    i2 = jnp.min(jnp.where(ep2 == p2, lanef, float(ROUTE_LANES)), axis=-1, keepdims=True)
    psum = p1 + p2
    gate1 = g_p * p1 / psum
    gate2 = g_p * p2 / psum
    return jnp.where(lane == 0, i1 - float(N_GROUPS),
                     jnp.where(lane == 1, i2 - float(N_GROUPS),
                               jnp.where(lane == 2, gate1,
                                         jnp.where(lane == 3, gate2, 0.0))))


N_ROW_TILES = NTOK // ROW_TILE
TILES_PER_BATCH = T // ROW_TILE


def _proj_kernel(ya_ref, yb_ref, w_ref, lng_ref, lnb_ref, x_ref, modb_ref, modc_ref, g2_ref,
                 rwh_ref, rwl_ref, rb_ref, xo_ref, hp_ref, ri_ref,
                 lhs_s, out_a, out_b, hh_s, hl_s, vec_s, *, even):
    s = pl.program_id(0)
    pieces = ROW_TILE // NM_CHUNK

    @pl.when(s == 0)
    def _():
        out_b[...] = jnp.zeros_like(out_b)

    is_ctx = jnp.maximum(s - 1, 0) % TILES_PER_BATCH == 0
    mod = jnp.where(is_ctx, modc_ref[...], modb_ref[...])
    vec_s[0:1, :] = mod[2:3, :]
    vec_s[1:2, :] = g2_ref[...] * (1.0 + mod[4:5, :])
    vec_s[2:3, :] = mod[3:4, :]

    def step(mm_out, epi_in):
        for c in range(pieces):
            rows = pl.ds(c * NM_CHUNK, NM_CHUNK)
            xn = x_ref[rows, :] + vec_s[0:1, :] * epi_in[rows, :]
            xo_ref[rows, :] = xn
            inv = lax.rsqrt(jnp.mean(xn * xn, axis=-1, keepdims=True) + RMS_EPS)
            h2 = xn * inv * vec_s[1:2, :] + vec_s[2:3, :]
            hp_ref[rows, :] = h2
            hb = h2.astype(BF16)
            hh_s[rows, :] = hb
            hl_s[rows, :] = (h2 - hb.astype(F32)).astype(BF16)
        hh = hh_s[...]
        logits = (jnp.dot(hh, rwh_ref[...], preferred_element_type=F32)
                  + jnp.dot(hl_s[...], rwh_ref[...], preferred_element_type=F32)
                  + jnp.dot(hh, rwl_ref[...], preferred_element_type=F32)) + rb_ref[...]
        ri_ref[...] = _route(logits)

        if even:
            for c in range(pieces):
                rows = pl.ds(c * NM_CHUNK, NM_CHUNK)
                zc = yb_ref[rows, :].astype(F32)
                dz = zc - jnp.mean(zc, axis=-1, keepdims=True)
                var = jnp.mean(dz * dz, axis=-1, keepdims=True)
                z = dz * lax.rsqrt(var + LN_EPS) * lng_ref[...] + lnb_ref[...]
                lhs_s[rows, pl.ds(HALF, HALF)] = (z * _sigmoid(z)).astype(BF16)
        else:
            lhs_s[:, pl.ds(HALF, HALF)] = yb_ref[...]
        lhs_s[:, pl.ds(0, HALF)] = ya_ref[...]
        for n in range(D // NM_TN):
            cols = pl.ds(n * NM_TN, NM_TN)
            mm_out[:, cols] = jnp.dot(lhs_s[...], w_ref[:, cols], preferred_element_type=F32)

    @pl.when(s % 2 == 0)
    def _():
        step(out_a, out_b)

    @pl.when(s % 2 == 1)
    def _():
        step(out_b, out_a)


def _proj(ya, yb, w_bf, widx, ln_g, ln_b, xc, mods, layer, g2, rw_hi, rw_lo, rb, even):
    def tile_of(t):
        return t // TILES_PER_BATCH, t % TILES_PER_BATCH

    def ahead(width):
        return pl.BlockSpec((None, ROW_TILE, width),
                            lambda s: (*tile_of(jnp.minimum(s, N_ROW_TILES - 1)), 0))

    def behind(width):
        return pl.BlockSpec((None, ROW_TILE, width), lambda s: (*tile_of(jnp.maximum(s - 1, 0)), 0))

    const = lambda *shape: pl.BlockSpec((None,) + shape, lambda s: (widx,) + (0,) * len(shape))
    per_layer = lambda *shape: pl.BlockSpec((None,) + shape, lambda s: (layer,) + (0,) * len(shape))
    return pl.pallas_call(
        functools.partial(_proj_kernel, even=even),
        grid=(N_ROW_TILES + 1,),
        in_specs=[ahead(HALF), ahead(HALF),
                  const(D, D), const(1, HALF), const(1, HALF),
                  behind(D),
                  pl.BlockSpec((None, None, 6, D),
                               lambda s: (layer, jnp.maximum(s - 1, 0) // TILES_PER_BATCH, 0, 0)),
                  pl.BlockSpec((None, None, 6, D), lambda s: (layer, B, 0, 0)),
                  per_layer(1, D), per_layer(D, ROUTE_LANES), per_layer(D, ROUTE_LANES),
                  per_layer(1, ROUTE_LANES)],
        out_specs=[behind(D), behind(D), behind(ROUTE_LANES)],
        out_shape=[jax.ShapeDtypeStruct((B, T, D), F32),
                   jax.ShapeDtypeStruct((B, T, D), F32),
                   jax.ShapeDtypeStruct((B, T, ROUTE_LANES), F32)],
        scratch_shapes=[pltpu.VMEM((ROW_TILE, D), BF16),
                        pltpu.VMEM((ROW_TILE, D), F32),
                        pltpu.VMEM((ROW_TILE, D), F32),
                        pltpu.VMEM((ROW_TILE, D), BF16),
                        pltpu.VMEM((ROW_TILE, D), BF16),
                        pltpu.VMEM((8, D), F32)],
        compiler_params=pltpu.CompilerParams(
            dimension_semantics=("arbitrary",), vmem_limit_bytes=VMEM_LIMIT),
        name="out_proj_router",
    )(ya, yb, w_bf, ln_g, ln_b, xc, mods, mods, g2, rw_hi, rw_lo, rb)


def _positions_kernel(ri_ref, pos_ref, base_ref):
    ntiles = NTOK // ROW_TILE
    lanef = lax.broadcasted_iota(I32, (ROW_TILE, ROUTE_LANES), 1).astype(F32)

    def count(i, acc):
        r = ri_ref[pl.ds(pl.multiple_of(i * ROW_TILE, ROW_TILE), ROW_TILE), :]
        hit = (lanef == r[:, 0:1]) | (lanef == r[:, 1:2])
        return acc + jnp.sum(jnp.where(hit, 1.0, 0.0), axis=0, keepdims=True)

    cnt = lax.fori_loop(0, ntiles, count, jnp.zeros((1, ROUTE_LANES), F32))
    ri = lax.broadcasted_iota(I32, (ROUTE_LANES, ROUTE_LANES), 0)
    ci = lax.broadcasted_iota(I32, (ROUTE_LANES, ROUTE_LANES), 1)
    before = jnp.where(ri < ci, 1.0, 0.0)
    cnt8 = jnp.broadcast_to(cnt, (8, ROUTE_LANES))
    base8 = jnp.dot(cnt8, before, preferred_element_type=F32, precision=HIGHEST)
    base_ref[...] = base8.astype(I32)
    base_col = jnp.broadcast_to(base8[0:1, :], (ROUTE_LANES, ROUTE_LANES)).T[:, 0:1]

    ti = lax.broadcasted_iota(I32, (ROW_TILE, ROW_TILE), 0)
    tj = lax.broadcasted_iota(I32, (ROW_TILE, ROW_TILE), 1)
    earlier = jnp.where(ti < tj, 1.0, 0.0).astype(BF16)
    subf = lax.broadcasted_iota(I32, (ROUTE_LANES, ROW_TILE), 0).astype(F32)

    def place(i, carry):
        start = pl.multiple_of(i * ROW_TILE, ROW_TILE)
        rt = ri_ref[pl.ds(start, ROW_TILE), :].T
        e0 = rt[0:1, :]
        e1 = rt[1:2, :]
        hit0 = subf == e0
        hit1 = subf == e1
        hits = jnp.where(hit0 | hit1, 1.0, 0.0)
        rank = jnp.dot(hits.astype(BF16), earlier, preferred_element_type=F32) + carry + base_col
        pos_ref[0:1, pl.ds(start, ROW_TILE)] = jnp.sum(
            jnp.where(hit0, rank, 0.0), axis=0, keepdims=True).astype(I32)
        pos_ref[1:2, pl.ds(start, ROW_TILE)] = jnp.sum(
            jnp.where(hit1, rank, 0.0), axis=0, keepdims=True).astype(I32)
        return carry + jnp.sum(hits, axis=1, keepdims=True)

    lax.fori_loop(0, ntiles, place, jnp.zeros((ROUTE_LANES, 1), F32))


def _positions(ri_flat):
    return pl.pallas_call(
        _positions_kernel,
        out_shape=[jax.ShapeDtypeStruct((2, NTOK), I32),
                   jax.ShapeDtypeStruct((8, ROUTE_LANES), I32)],
        compiler_params=pltpu.CompilerParams(vmem_limit_bytes=VMEM_LIMIT),
        name="moe_positions",
    )(ri_flat)


INVERT_UNROLL = 8


def _items_kernel(base_ref, pos_ref, it_ref, src_ref):
    def init(i, carry):
        it_ref[ITEM_NEW + i] = 0
        it_ref[ITEM_SLOT + i] = 0
        it_ref[ITEM_NEXT + i] = -1
        return carry

    lax.fori_loop(0, N_ITEMS, init, 0)

    def per_expert(e, carry):
        n, run, run_start = carry
        s = base_ref[e]
        t = base_ref[e + 1]
        first = lax.shift_right_logical(s, 8)
        last = lax.shift_right_logical(jnp.maximum(t - 1, 0), 8)
        used = t > s
        ntile = jnp.where(used, last - first + 1, 0)

        def per_tile(k, n2):
            tile = first + k
            it_ref[n2] = tile
            it_ref[N_ITEMS + n2] = e
            it_ref[2 * N_ITEMS + n2] = jnp.maximum(s - tile * ROW_TILE, 0)
            it_ref[3 * N_ITEMS + n2] = jnp.minimum(t - tile * ROW_TILE, ROW_TILE)
            return n2 + 1

        n_end = lax.fori_loop(0, ntile, per_tile, n)
        it_ref[ITEM_NEW + n] = jnp.where(used, 1, it_ref[ITEM_NEW + n])
        it_ref[ITEM_SLOT + n] = jnp.where(used, run & 1, it_ref[ITEM_SLOT + n])
        link = used & (run > 0)
        it_ref[ITEM_NEXT + run_start] = jnp.where(link, e, it_ref[ITEM_NEXT + run_start])
        return n_end, run + jnp.where(used, 1, 0), jnp.where(used, n, run_start)

    n_used, _, _ = lax.fori_loop(0, N_EXPERTS, per_expert,
                                 (jnp.int32(0), jnp.int32(0), jnp.int32(0)))
    last_e = it_ref[N_ITEMS + n_used - 1]

    def pad(i, carry):
        it_ref[i] = N_SORT_TILES - 1
        it_ref[N_ITEMS + i] = last_e
        it_ref[2 * N_ITEMS + i] = 0
        it_ref[3 * N_ITEMS + i] = 0
        return carry

    lax.fori_loop(n_used, N_ITEMS, pad, 0)

    def invert(g, carry):
        for k in range(INVERT_UNROLL):
            t = g * INVERT_UNROLL + k
            src_ref[pos_ref[t]] = t
            src_ref[pos_ref[NTOK + t]] = t
        return carry

    lax.fori_loop(0, NTOK // INVERT_UNROLL, invert, 0)


def _items(base_flat, pos_flat):
    smem = pl.BlockSpec(memory_space=pltpu.SMEM)
    return pl.pallas_call(
        _items_kernel,
        in_specs=[smem, smem],
        out_specs=[smem, smem],
        out_shape=[jax.ShapeDtypeStruct((7 * N_ITEMS,), I32),
                   jax.ShapeDtypeStruct((NSLOT,), I32)],
        name="moe_items",
    )(base_flat, pos_flat)


ROW_DMA_UNROLL = 8


def _gmm_kernel(it_ref, src_ref, h_hbm, w1_hbm, w3_hbm, w2_hbm, o_ref,
                xbuf_a, xbuf_b, w1_raw, w3_raw, w2_raw, w1_bf, w3_bf, w2_bf, sems, row_sems,
                *, layer):
    i = pl.program_id(0)
    tile = it_ref[i]
    expert = it_ref[N_ITEMS + i]
    lo = it_ref[2 * N_ITEMS + i]
    hi = it_ref[3 * N_ITEMS + i]
    slot = it_ref[ITEM_SLOT + i]
    nxt = it_ref[ITEM_NEXT + i]
    prev_tile = it_ref[jnp.maximum(i - 1, 0)]
    xbufs = (xbuf_a, xbuf_b)

    def gather_rows(tl, p, unroll):
        def rows(g, carry):
            for k in range(unroll):
                r = g * unroll + k
                pltpu.make_async_copy(h_hbm.at[pl.ds(src_ref[tl * ROW_TILE + r], 1)],
                                      xbufs[p].at[pl.ds(r, 1)], row_sems.at[p]).start()
            return carry

        if unroll == ROW_TILE:
            rows(0, 0)
        else:
            lax.fori_loop(0, ROW_TILE // unroll, rows, 0)

    def weight_copies(e, s):
        return (pltpu.make_async_copy(w1_hbm.at[layer, e], w1_raw.at[s], sems.at[s, 0]),
                pltpu.make_async_copy(w3_hbm.at[layer, e], w3_raw.at[s], sems.at[s, 1]),
                pltpu.make_async_copy(w2_hbm.at[layer, e], w2_raw.at[s], sems.at[s, 2]))

    @pl.when(it_ref[ITEM_NEW + i] == 1)
    def _():
        @pl.when(i == 0)
        def _():
            for cp in weight_copies(expert, slot):
                cp.start()

        for cp in weight_copies(expert, slot):
            cp.wait()

        @pl.when(nxt >= 0)
        def _():
            for cp in weight_copies(nxt, 1 - slot):
                cp.start()

        w1_bf[...] = w1_raw[slot].astype(BF16)
        w3_bf[...] = w3_raw[slot].astype(BF16)
        w2_bf[...] = w2_raw[slot].astype(BF16)

    new_tile = (i == 0) | (tile != prev_tile)

    def tile_work(p):
        cur = xbufs[p]

        @pl.when(new_tile)
        def _():
            @pl.when(i == 0)
            def _():
                gather_rows(tile, p, ROW_DMA_UNROLL)

            @pl.when(tile + 1 < N_SORT_TILES)
            def _():
                gather_rows(tile + 1, 1 - p, ROW_TILE)

            pltpu.make_async_copy(cur, cur, row_sems.at[p]).wait()
            o_ref[...] = jnp.zeros_like(o_ref)

        @pl.when(hi > lo)
        def _():
            xb = cur[...].astype(BF16)
            a = jnp.dot(xb, w1_bf[...], preferred_element_type=F32)
            b = jnp.dot(xb, w3_bf[...], preferred_element_type=F32)
            rows = lax.broadcasted_iota(I32, (ROW_TILE, 1), 0)
            hid = jnp.where((rows >= lo) & (rows < hi), a * _sigmoid(a) * b, 0.0)
            o_ref[...] += jnp.dot(hid.astype(BF16), w2_bf[...], preferred_element_type=F32)

    for p in range(2):
        pl.when(tile & 1 == p)(functools.partial(tile_work, p))


def _gmm(items, src, h_flat, w1, w3, w2, layer):
    hbm = pl.BlockSpec(memory_space=pl.ANY)
    return pl.pallas_call(
        functools.partial(_gmm_kernel, layer=layer),
        grid_spec=pltpu.PrefetchScalarGridSpec(
            num_scalar_prefetch=2,
            grid=(N_ITEMS,),
            in_specs=[hbm, hbm, hbm, hbm],
            out_specs=pl.BlockSpec((ROW_TILE, D), lambda i, it, src: (it[i], 0)),
            scratch_shapes=[pltpu.VMEM((ROW_TILE, D), F32),
                            pltpu.VMEM((ROW_TILE, D), F32),
                            pltpu.VMEM((2, D, EXPERT_FF), F32),
                            pltpu.VMEM((2, D, EXPERT_FF), F32),
                            pltpu.VMEM((2, EXPERT_FF, D), F32),
                            pltpu.VMEM((D, EXPERT_FF), BF16),
                            pltpu.VMEM((D, EXPERT_FF), BF16),
                            pltpu.VMEM((EXPERT_FF, D), BF16),
                            pltpu.SemaphoreType.DMA((2, 3)),
                            pltpu.SemaphoreType.DMA((2,))],
        ),
        out_shape=jax.ShapeDtypeStruct((NSLOT, D), F32),
        compiler_params=pltpu.CompilerParams(
            dimension_semantics=("arbitrary",), vmem_limit_bytes=VMEM_LIMIT),
        name="moe_experts",
    )(items, src, h_flat, w1, w3, w2)


def _combine_kernel(pos_ref, ys_hbm, x_ref, ri_ref, modb_ref, modc_ref, fg_ref, o_ref,
                    buf_a, buf_b, gate_s, sems, *, final, tiles_per_b, m_off):
    i = pl.program_id(0)
    ntiles = pl.num_programs(0)
    bufs = (buf_a, buf_b)

    def issue(tile_i, p, unroll):
        t0 = (tile_i // tiles_per_b) * T + (tile_i % tiles_per_b + m_off) * ROW_TILE

        def rows(g, carry):
            for k in range(unroll):
                r = g * unroll + k
                for s in range(2):
                    pltpu.make_async_copy(ys_hbm.at[pl.ds(pos_ref[s * NTOK + t0 + r], 1)],
                                          bufs[p].at[s, pl.ds(r, 1)], sems.at[p]).start()
            return carry

        if unroll == ROW_TILE:
            rows(0, 0)
        else:
            lax.fori_loop(0, ROW_TILE // unroll, rows, 0)

    @pl.when(i == 0)
    def _():
        issue(0, 0, ROW_DMA_UNROLL)

    m = i % tiles_per_b + m_off
    gate_s[...] = jnp.where(m == 0, modc_ref[5:6, :], modb_ref[5:6, :])

    def tile_work(p):
        cur = bufs[p]
        more = i + 1 < ntiles

        if not final:
            @pl.when(more)
            def _():
                issue(i + 1, 1 - p, ROW_TILE)

        pltpu.make_async_copy(cur, cur, sems.at[p]).wait()

        def gated_residual():
            for c in range(ROW_TILE // NM_CHUNK):
                rows = pl.ds(c * NM_CHUNK, NM_CHUNK)
                g0 = ri_ref[rows, 2:3]
                g1 = ri_ref[rows, 3:4]
                xn = x_ref[rows, :] + gate_s[...] * (g0 * cur[0, rows, :] + g1 * cur[1, rows, :])
                if final:
                    inv = lax.rsqrt(jnp.mean(xn * xn, axis=-1, keepdims=True) + RMS_EPS)
                    xn = xn * inv * fg_ref[...]
                o_ref[rows, :] = xn

        if final:
            @pl.when(more)
            def _():
                issue(i + 1, 1 - p, ROW_TILE)
                gated_residual()

            @pl.when(jnp.logical_not(more))
            def _():
                gated_residual()
        else:
            gated_residual()

    for p in range(2):
        pl.when(i % 2 == p)(functools.partial(tile_work, p))


def _combine(pos_flat, ys, xc, ri, mods, layer, final_g, final):
    if final:
        tiles_per_b, m_off, rows_out = SEQ // ROW_TILE, CTX // ROW_TILE, SEQ
    else:
        tiles_per_b, m_off, rows_out = T // ROW_TILE, 0, T
    tok = lambda width: pl.BlockSpec(
        (None, ROW_TILE, width), lambda i, pos: (i // tiles_per_b, i % tiles_per_b + m_off, 0))
    return pl.pallas_call(
        functools.partial(_combine_kernel, final=final, tiles_per_b=tiles_per_b, m_off=m_off),
        grid_spec=pltpu.PrefetchScalarGridSpec(
            num_scalar_prefetch=1,
            grid=(B * tiles_per_b,),
            in_specs=[
                pl.BlockSpec(memory_space=pl.ANY),
                tok(D),
                tok(ROUTE_LANES),
                pl.BlockSpec((None, None, 6, D), lambda i, pos: (layer, i // tiles_per_b, 0, 0)),
                pl.BlockSpec((None, None, 6, D), lambda i, pos: (layer, B, 0, 0)),
                pl.BlockSpec((1, D), lambda i, pos: (0, 0)),
            ],
            out_specs=pl.BlockSpec((None, ROW_TILE, D),
                                   lambda i, pos: (i // tiles_per_b, i % tiles_per_b, 0)),
            scratch_shapes=[pltpu.VMEM((2, ROW_TILE, D), F32),
                            pltpu.VMEM((2, ROW_TILE, D), F32),
                            pltpu.VMEM((1, D), F32),
                            pltpu.SemaphoreType.DMA((2,))],
        ),
        out_shape=jax.ShapeDtypeStruct((B, rows_out, D), F32),
        compiler_params=pltpu.CompilerParams(
            dimension_semantics=("arbitrary",), vmem_limit_bytes=VMEM_LIMIT),
        name="moe_combine",
    )(pos_flat, ys, xc, ri, mods, mods, final_g)


def kernel(x, c, ctx, c_ctx, ada_w, ada_b, norm1_g, norm2_g, ev_w_in, ev_conv_a, ev_dw_w, ev_dw_b,
           ev_ln_g, ev_ln_b, ev_w_out, od_w_in, od_rpb, od_pool_w, od_pool_scale, od_w_out,
           moe_rg_w, moe_rg_b, moe_re_w, moe_re_b, moe_w1, moe_w3, moe_w2, final_g):
    xc = jnp.concatenate([ctx, x], axis=1)
    c8 = jnp.concatenate([c, c_ctx[None, :], jnp.zeros((8 - B - 1, D), F32)], axis=0)
    mods = _ada(c8, ada_w, ada_b).reshape(DEPTH, 8, 6, D)

    ev_w_in_bf = ev_w_in.astype(BF16)
    od_w_in_bf = od_w_in.astype(BF16)
    ev_w_out_bf = ev_w_out.astype(BF16)
    od_w_out_bf = od_w_out.astype(BF16)
    pad = ROUTE_LANES - N_GROUPS - N_EXPERTS
    rw = jnp.concatenate([moe_rg_w, moe_re_w, jnp.zeros((DEPTH, D, pad), F32)], axis=-1)
    rw_hi = rw.astype(BF16)
    rw_lo = (rw - rw_hi.astype(F32)).astype(BF16)
    rb = jnp.concatenate([moe_rg_b, moe_re_b, jnp.zeros((DEPTH, pad), F32)],
                         axis=-1).reshape(DEPTH, 1, ROUTE_LANES)
    final_g2 = final_g.reshape(1, D)
    norm1_g = norm1_g.reshape(DEPTH, 1, D)
    norm2_g = norm2_g.reshape(DEPTH, 1, D)
    ev_ln_g = ev_ln_g.reshape(-1, 1, HALF)
    ev_ln_b = ev_ln_b.reshape(-1, 1, HALF)

    for i in range(DEPTH):
        j = i // 2
        last = i == DEPTH - 1
        if i % 2 == 1:
            u = _norm_mm(xc, mods, i, norm1_g, od_w_in_bf, j)
            ya = _attention(u, od_rpb.reshape(-1), j)
            yb = _pool(u, od_pool_w, od_pool_scale, j)
            xc, hp, ri = _proj(ya, yb, od_w_out_bf, j, ev_ln_g, ev_ln_b, xc, mods, i, norm2_g,
                               rw_hi, rw_lo, rb, even=False)
        else:
            u = _norm_mm(xc, mods, i, norm1_g, ev_w_in_bf, j)
            ya, yb = _conv(u, ev_conv_a, ev_dw_w, ev_dw_b, j)
            xc, hp, ri = _proj(ya, yb, ev_w_out_bf, j, ev_ln_g, ev_ln_b, xc, mods, i, norm2_g,
                               rw_hi, rw_lo, rb, even=True)
        pos, base = _positions(ri.reshape(NTOK, ROUTE_LANES))
        pos_flat = pos.reshape(2 * NTOK)
        items, src = _items(base[0, :N_EXPERTS + 1 + 7], pos_flat)
        ys = _gmm(items, src, hp.reshape(NTOK, D), moe_w1, moe_w3, moe_w2, i)
        xc = _combine(pos_flat, ys, xc, ri, mods, i, final_g2, last)
    return xc
```

```python
import functools

import numpy as np
import jax
import jax.numpy as jnp
from jax import lax
from jax.experimental import pallas as pl
from jax.experimental.pallas import tpu as pltpu

F32 = jnp.float32
BF16 = jnp.bfloat16
I32 = jnp.int32
HIGHEST = lax.Precision.HIGHEST

D = 2048
B = 4
SEQ = 2048
CTX = 256
T = CTX + SEQ
NTOK = B * T
DEPTH = 4
GRID_W = 64
GRID_ROWS = SEQ // GRID_W
HALF = D // 2
SC_W = 1024
CF_KERNEL = 31
SC_WIDTH = 3
NA_HEADS = 8
NA_DH = 128
NA_ROWS = 8
NA_COLS = 16
N_POOL = 4
POOL_GC = 256
N_EXPERTS = 32
N_GROUPS = 4
EPG = 8
EXPERT_FF = 256
RMS_EPS = 1e-6
LN_EPS = 1e-5
NEG_INF = -1e30

ROW_TILE = 256
NSLOT = 2 * NTOK
N_SORT_TILES = NSLOT // ROW_TILE
N_ITEMS = 104
ITEM_NEW, ITEM_SLOT, ITEM_NEXT = 4 * N_ITEMS, 5 * N_ITEMS, 6 * N_ITEMS
NM_TN = 1024
NM_CHUNK = 16
ATT_QROWS = 4
ATT_Q = ATT_QROWS * GRID_W
ATT_KROWS = 12
ATT_K = ATT_KROWS * GRID_W
CONV_CW = 256
CONV_PAD = 16
CONV_ROWS = 32
VMEM_LIMIT = 56 * 1024 * 1024


def _sigmoid(v):
    return 1.0 / (1.0 + jnp.exp(-v))


ADA_TN = 1024


def _ada_kernel(c_ref, w_ref, b_ref, o_ref):
    cv = c_ref[...]
    s = cv * _sigmoid(cv)
    s_hi = s.astype(BF16)
    s_lo = (s - s_hi.astype(F32)).astype(BF16)
    w = w_ref[0]
    w_hi = w.astype(BF16)
    w_lo = (w - w_hi.astype(F32)).astype(BF16)
    o_ref[0] = (jnp.dot(s_hi, w_hi, preferred_element_type=F32)
                + jnp.dot(s_lo, w_hi, preferred_element_type=F32)
                + jnp.dot(s_hi, w_lo, preferred_element_type=F32)) + b_ref[0]


def _ada(c8, ada_w, ada_b):
    n = ada_w.shape[-1]
    return pl.pallas_call(
        _ada_kernel,
        grid=(DEPTH, n // ADA_TN),
        in_specs=[
            pl.BlockSpec((8, D), lambda l, j: (0, 0)),
            pl.BlockSpec((1, D, ADA_TN), lambda l, j: (l, 0, j)),
            pl.BlockSpec((1, 1, ADA_TN), lambda l, j: (l, 0, j)),
        ],
        out_specs=pl.BlockSpec((1, 8, ADA_TN), lambda l, j: (l, 0, j)),
        out_shape=jax.ShapeDtypeStruct((DEPTH, 8, n), F32),
        compiler_params=pltpu.CompilerParams(
            dimension_semantics=("arbitrary", "arbitrary"), vmem_limit_bytes=VMEM_LIMIT),
        name="ada",
    )(c8, ada_w, ada_b.reshape(DEPTH, 1, n))


def _norm_mm_kernel(x_ref, modb_ref, modc_ref, g_ref, w_ref, o_ref, h_ref, gain_ref, shift_ref):
    is_ctx = pl.program_id(1) == 0
    gain_ref[...] = g_ref[...] * (1.0 + jnp.where(is_ctx, modc_ref[1:2, :], modb_ref[1:2, :]))
    shift_ref[...] = jnp.where(is_ctx, modc_ref[0:1, :], modb_ref[0:1, :])
    for c in range(ROW_TILE // NM_CHUNK):
        rows = pl.ds(c * NM_CHUNK, NM_CHUNK)
        xv = x_ref[rows, :]
        inv = lax.rsqrt(jnp.mean(xv * xv, axis=-1, keepdims=True) + RMS_EPS)
        h_ref[rows, :] = (xv * inv * gain_ref[...] + shift_ref[...]).astype(BF16)
    for n in range(o_ref.shape[1] // NM_TN):
        cols = pl.ds(n * NM_TN, NM_TN)
        o_ref[:, cols] = jnp.dot(h_ref[...], w_ref[:, cols],
                                 preferred_element_type=F32).astype(BF16)


def _norm_mm(xc, mods, layer, g, w_bf, widx):
    n_out = w_bf.shape[-1]
    return pl.pallas_call(
        _norm_mm_kernel,
        grid=(B, T // ROW_TILE),
        in_specs=[
            pl.BlockSpec((None, ROW_TILE, D), lambda b, m: (b, m, 0)),
            pl.BlockSpec((None, None, 6, D), lambda b, m: (layer, b, 0, 0)),
            pl.BlockSpec((None, None, 6, D), lambda b, m: (layer, B, 0, 0)),
            pl.BlockSpec((None, 1, D), lambda b, m: (layer, 0, 0)),
            pl.BlockSpec((None, D, n_out), lambda b, m: (widx, 0, 0), pipeline_mode=pl.Buffered(1)),
        ],
        out_specs=pl.BlockSpec((None, ROW_TILE, n_out), lambda b, m: (b, m, 0)),
        out_shape=jax.ShapeDtypeStruct((B, T, n_out), BF16),
        scratch_shapes=[pltpu.VMEM((ROW_TILE, D), BF16),
                        pltpu.VMEM((1, D), F32), pltpu.VMEM((1, D), F32)],
        compiler_params=pltpu.CompilerParams(
            dimension_semantics=("arbitrary", "arbitrary"), vmem_limit_bytes=VMEM_LIMIT),
        name="norm_mm",
    )(xc, mods, mods, g, w_bf)


def _depthwise_taps(win, taps, width):
    nwin = CONV_ROWS + 2 * CONV_PAD
    acc = jnp.zeros((CONV_ROWS, win.shape[1]), F32)
    for res in range(8):
        offsets = [o for o in range(res, nwin - CONV_ROWS + 1, 8)
                   if 0 <= o - CONV_PAD + width // 2 < width]
        if not offsets:
            continue
        rolled = win if res == 0 else pltpu.roll(win, nwin - res, 0)
        for o in offsets:
            k = o - CONV_PAD + width // 2
            acc = acc + rolled[o - res:o - res + CONV_ROWS, :] * taps[k:k + 1, :]
    return acc


def _conv_kernel(bg_ref, cg_ref, xa_ref, ga_ref, gb_ref, ca_ref, dw_ref, dwb_ref,
                 ya_ref, zc_ref, vs_ref, zs_ref):
    ca = ca_ref[...]
    dw = dw_ref[...]
    dwb = dwb_ref[...]
    zeros_pad = jnp.zeros((CONV_PAD, CONV_CW), F32)

    def run_sequence(seq0, length):
        vs_ref[pl.ds(0, CONV_PAD), :] = zeros_pad
        zs_ref[pl.ds(0, CONV_PAD), :] = zeros_pad
        vs_ref[pl.ds(CONV_PAD + length, CONV_PAD), :] = zeros_pad
        zs_ref[pl.ds(CONV_PAD + length, CONV_PAD), :] = zeros_pad

        def fill(i, carry):
            r = pl.multiple_of(i * ROW_TILE, ROW_TILE)
            src = pl.ds(seq0 + r, ROW_TILE)
            dst = pl.ds(CONV_PAD + r, ROW_TILE)
            vs_ref[dst, :] = cg_ref[src, :].astype(F32) * xa_ref[src, :].astype(F32)
            zs_ref[dst, :] = ga_ref[src, :].astype(F32) * _sigmoid(gb_ref[src, :].astype(F32))
            return carry

        lax.fori_loop(0, length // ROW_TILE, fill, 0)

        def conv(i, carry):
            r = pl.multiple_of(i * CONV_ROWS, CONV_ROWS)
            win = pl.ds(r, CONV_ROWS + 2 * CONV_PAD)
            out = pl.ds(seq0 + r, CONV_ROWS)
            acc3 = _depthwise_taps(vs_ref[win, :], ca, SC_WIDTH)
            ya_ref[out, :] = (bg_ref[out, :].astype(F32) * acc3).astype(BF16)
            acc = _depthwise_taps(zs_ref[win, :], dw, CF_KERNEL)
            zc_ref[out, :] = (acc + dwb).astype(BF16)
            return carry

        lax.fori_loop(0, length // CONV_ROWS, conv, 0)

    run_sequence(0, CTX)
    run_sequence(CTX, SEQ)


def _conv(u, conv_a, dw_w, dw_b, j):
    nblk = SC_W // CONV_CW
    part = lambda k: pl.BlockSpec((None, T, CONV_CW), lambda b, cb: (b, 0, k * nblk + cb))
    out_spec = pl.BlockSpec((None, T, CONV_CW), lambda b, cb: (b, 0, cb))
    return pl.pallas_call(
        _conv_kernel,
        grid=(B, nblk),
        in_specs=[part(0), part(1), part(2), part(3), part(4),
                  pl.BlockSpec((None, SC_WIDTH, CONV_CW), lambda b, cb: (j, 0, cb)),
                  pl.BlockSpec((None, CF_KERNEL, CONV_CW), lambda b, cb: (j, 0, cb)),
                  pl.BlockSpec((None, 1, CONV_CW), lambda b, cb: (j, 0, cb))],
        out_specs=[out_spec, out_spec],
        out_shape=[jax.ShapeDtypeStruct((B, T, HALF), BF16)] * 2,
        scratch_shapes=[pltpu.VMEM((SEQ + 2 * CONV_PAD, CONV_CW), F32)] * 2,
        compiler_params=pltpu.CompilerParams(
            dimension_semantics=("arbitrary", "arbitrary"), vmem_limit_bytes=VMEM_LIMIT),
        name="conv_mixers",
    )(u, u, u, u, u, conv_a, dw_w, dw_b.reshape(dw_b.shape[0], 1, dw_b.shape[1]))


N_RPB_DR = 2 * NA_ROWS - 1
N_RPB_DC = 2 * NA_COLS - 1
ATT_MASKED = N_RPB_DR


def _att_block_geometry(blk):
    r0 = blk * ATT_QROWS
    return min(max(r0 - NA_ROWS // 2, 0), GRID_ROWS - ATT_KROWS)


def _softmax_pv(parts):
    mx = None
    for s, _ in parts:
        pm = jnp.max(s, axis=-1, keepdims=True)
        mx = pm if mx is None else jnp.maximum(mx, pm)
    den = None
    num = None
    for s, v in parts:
        e = jnp.exp(s - mx)
        ps = jnp.sum(e, axis=-1, keepdims=True)
        pv = jnp.dot(e.astype(BF16), v, preferred_element_type=F32)
        den = ps if den is None else den + ps
        num = pv if num is None else num + pv
    return num / den


def _att_kernel(rpb_ref, q_ref, k_ref, v_ref, o_ref, tile_ref, *, rpb_base):
    @pl.when(pl.program_id(1) == 0)
    def _():
        base = (rpb_base + pl.program_id(0)) * (N_RPB_DR * N_RPB_DC)
        ci = lax.broadcasted_iota(I32, (GRID_W, 2 * GRID_W), 0)
        wi = lax.broadcasted_iota(I32, (GRID_W, 2 * GRID_W), 1) & (GRID_W - 1)
        cs = jnp.clip(ci - NA_COLS // 2, 0, GRID_W - NA_COLS)
        dsel = jnp.where((wi >= cs) & (wi < cs + NA_COLS), wi - ci + (NA_COLS - 1), -1)
        masked = jnp.full((GRID_W, 2 * GRID_W), NEG_INF, F32)
        for dr in range(N_RPB_DR):
            t = masked
            for dc in range(N_RPB_DC):
                t = jnp.where(dsel == dc, rpb_ref[base + dr * N_RPB_DC + dc], t)
            tile_ref[dr] = t
        tile_ref[ATT_MASKED] = masked

    def block_bias(blk, kr0):
        low_half = lax.broadcasted_iota(I32, (GRID_W, 2 * GRID_W), 1) < GRID_W
        bands = []
        for qr in range(ATT_QROWS):
            r = blk * ATT_QROWS + qr
            rs = min(max(r - NA_ROWS // 2, 0), GRID_ROWS - NA_ROWS)
            tiles = []
            for jp in range(ATT_KROWS // 2):
                idx = [kr - r + (NA_ROWS - 1) if rs <= kr < rs + NA_ROWS else ATT_MASKED
                       for kr in (kr0 + 2 * jp, kr0 + 2 * jp + 1)]
                if idx[0] == idx[1]:
                    tiles.append(tile_ref[idx[0]])
                else:
                    tiles.append(jnp.where(low_half, tile_ref[idx[0]], tile_ref[idx[1]]))
            bands.append(jnp.concatenate(tiles, axis=1))
        return jnp.concatenate(bands, axis=0)

    scale = NA_DH ** -0.5
    nt = (((1,), (1,)), ((), ()))
    kc = k_ref[pl.ds(0, CTX), :]
    vc = v_ref[pl.ds(0, CTX), :]

    qc = q_ref[pl.ds(0, CTX), :]
    s_cc = lax.dot_general(qc, kc, nt, preferred_element_type=F32) * scale
    o_ref[pl.ds(0, CTX), :] = _softmax_pv([(s_cc, vc)]).astype(BF16)

    for blk in range(GRID_ROWS // ATT_QROWS):
        kr0 = _att_block_geometry(blk)
        qrows = pl.ds(CTX + blk * ATT_Q, ATT_Q)
        krows = pl.ds(CTX + kr0 * GRID_W, ATT_K)
        qb = q_ref[qrows, :]
        s_w = lax.dot_general(qb, k_ref[krows, :], nt, preferred_element_type=F32) * scale
        s_w = s_w + block_bias(blk, kr0)
        s_c = lax.dot_general(qb, kc, nt, preferred_element_type=F32) * scale
        o_ref[qrows, :] = _softmax_pv([(s_w, v_ref[krows, :]), (s_c, vc)]).astype(BF16)


def _attention(u, rpb_flat, j):
    col = lambda off: pl.BlockSpec((None, T, NA_DH), lambda h, b: (b, 0, off + h))
    return pl.pallas_call(
        functools.partial(_att_kernel, rpb_base=j * NA_HEADS),
        grid=(NA_HEADS, B),
        in_specs=[pl.BlockSpec(memory_space=pltpu.SMEM),
                  col(0), col(NA_HEADS), col(2 * NA_HEADS)],
        out_specs=pl.BlockSpec((None, T, NA_DH), lambda h, b: (b, 0, h)),
        out_shape=jax.ShapeDtypeStruct((B, T, HALF), BF16),
        scratch_shapes=[pltpu.VMEM((N_RPB_DR + 1, GRID_W, 2 * GRID_W), F32)],
        compiler_params=pltpu.CompilerParams(
            dimension_semantics=("arbitrary", "arbitrary"), vmem_limit_bytes=VMEM_LIMIT),
        name="neighbourhood_attention",
    )(rpb_flat, u, u, u)


POOL_KWIN = 512


def _pool_kernel(p_ref, pw_ref, ps_ref, o_ref):
    half = lax.shift_left(jnp.int32(1), pl.program_id(1))
    pw = pw_ref[...].astype(BF16)
    sc = ps_ref[...]

    def tile(row0, seq0, length, ks, kwin):
        t = (row0 - seq0) + lax.broadcasted_iota(I32, (ROW_TILE, 1), 0)
        s = ks + lax.broadcasted_iota(I32, (1, kwin), 1)
        lo = t - half
        hi = t + half - 1
        band = jnp.where((s >= lo) & (s <= hi), 1.0, 0.0).astype(BF16)
        ssum = jnp.dot(band, p_ref[pl.ds(seq0 + ks, kwin), :], preferred_element_type=F32)
        cnt = (jnp.minimum(hi, length - 1) - jnp.maximum(lo, 0) + 1).astype(F32)
        mix = ssum / cnt - p_ref[pl.ds(row0, ROW_TILE), :].astype(F32)
        y = jnp.dot(mix.astype(BF16), pw, preferred_element_type=F32) * sc
        o_ref[pl.ds(row0, ROW_TILE), :] = y.astype(BF16)

    tile(0, 0, CTX, 0, CTX)
    for jt in range(SEQ // ROW_TILE):
        ks = min(max(ROW_TILE * jt - (POOL_KWIN - ROW_TILE) // 2, 0), SEQ - POOL_KWIN)
        tile(CTX + ROW_TILE * jt, CTX, SEQ, ks, POOL_KWIN)


def _pool(u, pool_w, pool_scale, j):
    p_block0 = 3 * HALF // POOL_GC
    return pl.pallas_call(
        _pool_kernel,
        grid=(B, N_POOL),
        in_specs=[pl.BlockSpec((None, T, POOL_GC), lambda b, g: (b, 0, p_block0 + g)),
                  pl.BlockSpec((None, None, POOL_GC, POOL_GC), lambda b, g: (j, g, 0, 0)),
                  pl.BlockSpec((None, None, 1, POOL_GC), lambda b, g: (j, g, 0, 0))],
        out_specs=pl.BlockSpec((None, T, POOL_GC), lambda b, g: (b, 0, g)),
        out_shape=jax.ShapeDtypeStruct((B, T, HALF), BF16),
        compiler_params=pltpu.CompilerParams(
            dimension_semantics=("arbitrary", "arbitrary"), vmem_limit_bytes=VMEM_LIMIT),
        name="pool_mixer",
    )(u, pool_w, pool_scale.reshape(pool_scale.shape[0], N_POOL, 1, POOL_GC))


ROUTE_LANES = 128


def _route(logits):
    lane = lax.broadcasted_iota(I32, logits.shape, 1)
    lanef = lane.astype(F32)
    is_g = lane < N_GROUPS
    gl = jnp.where(is_g, logits, NEG_INF)
    ge = jnp.where(is_g, jnp.exp(gl - jnp.max(gl, axis=-1, keepdims=True)), 0.0)
    gp = ge / jnp.sum(ge, axis=-1, keepdims=True)
    g_p = jnp.max(gp, axis=-1, keepdims=True)
    g_idx = jnp.min(jnp.where(is_g & (gp == g_p), lanef, float(ROUTE_LANES)), axis=-1, keepdims=True)
    first = float(N_GROUPS) + float(EPG) * g_idx
    sel = (lanef >= first) & (lanef < first + float(EPG))
    el = jnp.where(sel, logits, NEG_INF)
    ee = jnp.where(sel, jnp.exp(el - jnp.max(el, axis=-1, keepdims=True)), 0.0)
    ep = jnp.where(sel, ee / jnp.sum(ee, axis=-1, keepdims=True), -1.0)
    p1 = jnp.max(ep, axis=-1, keepdims=True)
    i1 = jnp.min(jnp.where(ep == p1, lanef, float(ROUTE_LANES)), axis=-1, keepdims=True)
    ep2 = jnp.where(lanef == i1, -1.0, ep)
    p2 = jnp.max(ep2, axis=-1, keepdims=True)
    i2 = jnp.min(jnp.where(ep2 == p2, lanef, float(ROUTE_LANES)), axis=-1, keepdims=True)
    psum = p1 + p2
    gate1 = g_p * p1 / psum
    gate2 = g_p * p2 / psum
    return jnp.where(lane == 0, i1 - float(N_GROUPS),
                     jnp.where(lane == 1, i2 - float(N_GROUPS),
                               jnp.where(lane == 2, gate1,
                                         jnp.where(lane == 3, gate2, 0.0))))


N_ROW_TILES = NTOK // ROW_TILE
TILES_PER_BATCH = T // ROW_TILE


def _proj_kernel(ya_ref, yb_ref, w_ref, lng_ref, lnb_ref, x_ref, modb_ref, modc_ref, g2_ref,
                 rwh_ref, rwl_ref, rb_ref, xo_ref, hp_ref, ri_ref,
                 lhs_s, out_a, out_b, hh_s, hl_s, vec_s, *, even):
    s = pl.program_id(0)
    pieces = ROW_TILE // NM_CHUNK

    @pl.when(s == 0)
    def _():
        out_b[...] = jnp.zeros_like(out_b)

    is_ctx = jnp.maximum(s - 1, 0) % TILES_PER_BATCH == 0
    mod = jnp.where(is_ctx, modc_ref[...], modb_ref[...])
    vec_s[0:1, :] = mod[2:3, :]
    vec_s[1:2, :] = g2_ref[...] * (1.0 + mod[4:5, :])
    vec_s[2:3, :] = mod[3:4, :]

    def step(mm_out, epi_in):
        for c in range(pieces):
            rows = pl.ds(c * NM_CHUNK, NM_CHUNK)
            xn = x_ref[rows, :] + vec_s[0:1, :] * epi_in[rows, :]
            xo_ref[rows, :] = xn
            inv = lax.rsqrt(jnp.mean(xn * xn, axis=-1, keepdims=True) + RMS_EPS)
            h2 = xn * inv * vec_s[1:2, :] + vec_s[2:3, :]
            hp_ref[rows, :] = h2
            hb = h2.astype(BF16)
            hh_s[rows, :] = hb
            hl_s[rows, :] = (h2 - hb.astype(F32)).astype(BF16)
        hh = hh_s[...]
        logits = (jnp.dot(hh, rwh_ref[...], preferred_element_type=F32)
                  + jnp.dot(hl_s[...], rwh_ref[...], preferred_element_type=F32)
                  + jnp.dot(hh, rwl_ref[...], preferred_element_type=F32)) + rb_ref[...]
        ri_ref[...] = _route(logits)

        if even:
            for c in range(pieces):
                rows = pl.ds(c * NM_CHUNK, NM_CHUNK)
                zc = yb_ref[rows, :].astype(F32)
                dz = zc - jnp.mean(zc, axis=-1, keepdims=True)
                var = jnp.mean(dz * dz, axis=-1, keepdims=True)
                z = dz * lax.rsqrt(var + LN_EPS) * lng_ref[...] + lnb_ref[...]
                lhs_s[rows, pl.ds(HALF, HALF)] = (z * _sigmoid(z)).astype(BF16)
        else:
            lhs_s[:, pl.ds(HALF, HALF)] = yb_ref[...]
        lhs_s[:, pl.ds(0, HALF)] = ya_ref[...]
        for n in range(D // NM_TN):
            cols = pl.ds(n * NM_TN, NM_TN)
            mm_out[:, cols] = jnp.dot(lhs_s[...], w_ref[:, cols], preferred_element_type=F32)

    @pl.when(s % 2 == 0)
    def _():
        step(out_a, out_b)

    @pl.when(s % 2 == 1)
    def _():
        step(out_b, out_a)


def _proj(ya, yb, w_bf, widx, ln_g, ln_b, xc, mods, layer, g2, rw_hi, rw_lo, rb, even):
    def tile_of(t):
        return t // TILES_PER_BATCH, t % TILES_PER_BATCH

    def ahead(width):
        return pl.BlockSpec((None, ROW_TILE, width),
                            lambda s: (*tile_of(jnp.minimum(s, N_ROW_TILES - 1)), 0))

    def behind(width):
        return pl.BlockSpec((None, ROW_TILE, width), lambda s: (*tile_of(jnp.maximum(s - 1, 0)), 0))

    const = lambda *shape: pl.BlockSpec((None,) + shape, lambda s: (widx,) + (0,) * len(shape))
    per_layer = lambda *shape: pl.BlockSpec((None,) + shape, lambda s: (layer,) + (0,) * len(shape))
    return pl.pallas_call(
        functools.partial(_proj_kernel, even=even),
        grid=(N_ROW_TILES + 1,),
        in_specs=[ahead(HALF), ahead(HALF),
                  const(D, D), const(1, HALF), const(1, HALF),
                  behind(D),
                  pl.BlockSpec((None, None, 6, D),
                               lambda s: (layer, jnp.maximum(s - 1, 0) // TILES_PER_BATCH, 0, 0)),
                  pl.BlockSpec((None, None, 6, D), lambda s: (layer, B, 0, 0)),
                  per_layer(1, D), per_layer(D, ROUTE_LANES), per_layer(D, ROUTE_LANES),
                  per_layer(1, ROUTE_LANES)],
        out_specs=[behind(D), behind(D), behind(ROUTE_LANES)],
        out_shape=[jax.ShapeDtypeStruct((B, T, D), F32),
                   jax.ShapeDtypeStruct((B, T, D), F32),
                   jax.ShapeDtypeStruct((B, T, ROUTE_LANES), F32)],
        scratch_shapes=[pltpu.VMEM((ROW_TILE, D), BF16),
                        pltpu.VMEM((ROW_TILE, D), F32),
                        pltpu.VMEM((ROW_TILE, D), F32),
                        pltpu.VMEM((ROW_TILE, D), BF16),
                        pltpu.VMEM((ROW_TILE, D), BF16),
                        pltpu.VMEM((8, D), F32)],
        compiler_params=pltpu.CompilerParams(
            dimension_semantics=("arbitrary",), vmem_limit_bytes=VMEM_LIMIT),
        name="out_proj_router",
    )(ya, yb, w_bf, ln_g, ln_b, xc, mods, mods, g2, rw_hi, rw_lo, rb)


def _positions_kernel(ri_ref, pos_ref, base_ref):
    ntiles = NTOK // ROW_TILE
    lanef = lax.broadcasted_iota(I32, (ROW_TILE, ROUTE_LANES), 1).astype(F32)

    def count(i, acc):
        r = ri_ref[pl.ds(pl.multiple_of(i * ROW_TILE, ROW_TILE), ROW_TILE), :]
        hit = (lanef == r[:, 0:1]) | (lanef == r[:, 1:2])
        return acc + jnp.sum(jnp.where(hit, 1.0, 0.0), axis=0, keepdims=True)

    cnt = lax.fori_loop(0, ntiles, count, jnp.zeros((1, ROUTE_LANES), F32))
    ri = lax.broadcasted_iota(I32, (ROUTE_LANES, ROUTE_LANES), 0)
    ci = lax.broadcasted_iota(I32, (ROUTE_LANES, ROUTE_LANES), 1)
    before = jnp.where(ri < ci, 1.0, 0.0)
    cnt8 = jnp.broadcast_to(cnt, (8, ROUTE_LANES))
    base8 = jnp.dot(cnt8, before, preferred_element_type=F32, precision=HIGHEST)
    base_ref[...] = base8.astype(I32)
    base_col = jnp.broadcast_to(base8[0:1, :], (ROUTE_LANES, ROUTE_LANES)).T[:, 0:1]

    ti = lax.broadcasted_iota(I32, (ROW_TILE, ROW_TILE), 0)
    tj = lax.broadcasted_iota(I32, (ROW_TILE, ROW_TILE), 1)
    earlier = jnp.where(ti < tj, 1.0, 0.0).astype(BF16)
    subf = lax.broadcasted_iota(I32, (ROUTE_LANES, ROW_TILE), 0).astype(F32)

    def place(i, carry):
        start = pl.multiple_of(i * ROW_TILE, ROW_TILE)
        rt = ri_ref[pl.ds(start, ROW_TILE), :].T
        e0 = rt[0:1, :]
        e1 = rt[1:2, :]
        hit0 = subf == e0
        hit1 = subf == e1
        hits = jnp.where(hit0 | hit1, 1.0, 0.0)
        rank = jnp.dot(hits.astype(BF16), earlier, preferred_element_type=F32) + carry + base_col
        pos_ref[0:1, pl.ds(start, ROW_TILE)] = jnp.sum(
            jnp.where(hit0, rank, 0.0), axis=0, keepdims=True).astype(I32)
        pos_ref[1:2, pl.ds(start, ROW_TILE)] = jnp.sum(
            jnp.where(hit1, rank, 0.0), axis=0, keepdims=True).astype(I32)
        return carry + jnp.sum(hits, axis=1, keepdims=True)

    lax.fori_loop(0, ntiles, place, jnp.zeros((ROUTE_LANES, 1), F32))


def _positions(ri_flat):
    return pl.pallas_call(
        _positions_kernel,
        out_shape=[jax.ShapeDtypeStruct((2, NTOK), I32),
                   jax.ShapeDtypeStruct((8, ROUTE_LANES), I32)],
        compiler_params=pltpu.CompilerParams(vmem_limit_bytes=VMEM_LIMIT),
        name="moe_positions",
    )(ri_flat)


INVERT_UNROLL = 8


def _items_kernel(base_ref, pos_ref, it_ref, src_ref):
    def init(i, carry):
        it_ref[ITEM_NEW + i] = 0
        it_ref[ITEM_SLOT + i] = 0
        it_ref[ITEM_NEXT + i] = -1
        return carry

    lax.fori_loop(0, N_ITEMS, init, 0)

    def per_expert(e, carry):
        n, run, run_start = carry
        s = base_ref[e]
        t = base_ref[e + 1]
        first = lax.shift_right_logical(s, 8)
        last = lax.shift_right_logical(jnp.maximum(t - 1, 0), 8)
        used = t > s
        ntile = jnp.where(used, last - first + 1, 0)

        def per_tile(k, n2):
            tile = first + k
            it_ref[n2] = tile
            it_ref[N_ITEMS + n2] = e
            it_ref[2 * N_ITEMS + n2] = jnp.maximum(s - tile * ROW_TILE, 0)
            it_ref[3 * N_ITEMS + n2] = jnp.minimum(t - tile * ROW_TILE, ROW_TILE)
            return n2 + 1

        n_end = lax.fori_loop(0, ntile, per_tile, n)
        it_ref[ITEM_NEW + n] = jnp.where(used, 1, it_ref[ITEM_NEW + n])
        it_ref[ITEM_SLOT + n] = jnp.where(used, run & 1, it_ref[ITEM_SLOT + n])
        link = used & (run > 0)
        it_ref[ITEM_NEXT + run_start] = jnp.where(link, e, it_ref[ITEM_NEXT + run_start])
        return n_end, run + jnp.where(used, 1, 0), jnp.where(used, n, run_start)

    n_used, _, _ = lax.fori_loop(0, N_EXPERTS, per_expert,
                                 (jnp.int32(0), jnp.int32(0), jnp.int32(0)))
    last_e = it_ref[N_ITEMS + n_used - 1]

    def pad(i, carry):
        it_ref[i] = N_SORT_TILES - 1
        it_ref[N_ITEMS + i] = last_e
        it_ref[2 * N_ITEMS + i] = 0
        it_ref[3 * N_ITEMS + i] = 0
        return carry

    lax.fori_loop(n_used, N_ITEMS, pad, 0)

    def invert(g, carry):
        for k in range(INVERT_UNROLL):
            t = g * INVERT_UNROLL + k
            src_ref[pos_ref[t]] = t
            src_ref[pos_ref[NTOK + t]] = t
        return carry

    lax.fori_loop(0, NTOK // INVERT_UNROLL, invert, 0)


def _items(base_flat, pos_flat):
    smem = pl.BlockSpec(memory_space=pltpu.SMEM)
    return pl.pallas_call(
        _items_kernel,
        in_specs=[smem, smem],
        out_specs=[smem, smem],
        out_shape=[jax.ShapeDtypeStruct((7 * N_ITEMS,), I32),
                   jax.ShapeDtypeStruct((NSLOT,), I32)],
        name="moe_items",
    )(base_flat, pos_flat)


ROW_DMA_UNROLL = 8
WEIGHT_DMA_PRIORITY = 1


def _gmm_kernel(it_ref, src_ref, h_hbm, w1_hbm, w3_hbm, w2_hbm, o_ref,
                xbuf_a, xbuf_b, w1_raw, w3_raw, w2_raw, w1_bf, w3_bf, w2_bf, sems, row_sems,
                *, layer):
    i = pl.program_id(0)
    tile = it_ref[i]
    expert = it_ref[N_ITEMS + i]
    lo = it_ref[2 * N_ITEMS + i]
    hi = it_ref[3 * N_ITEMS + i]
    slot = it_ref[ITEM_SLOT + i]
    nxt = it_ref[ITEM_NEXT + i]
    prev_tile = it_ref[jnp.maximum(i - 1, 0)]
    xbufs = (xbuf_a, xbuf_b)

    def gather_rows(tl, p, unroll):
        def rows(g, carry):
            for k in range(unroll):
                r = g * unroll + k
                pltpu.make_async_copy(h_hbm.at[pl.ds(src_ref[tl * ROW_TILE + r], 1)],
                                      xbufs[p].at[pl.ds(r, 1)], row_sems.at[p]).start()
            return carry

        if unroll == ROW_TILE:
            rows(0, 0)
        else:
            lax.fori_loop(0, ROW_TILE // unroll, rows, 0)

    def weight_copies(e, s):
        return (pltpu.make_async_copy(w1_hbm.at[layer, e], w1_raw.at[s], sems.at[s, 0]),
                pltpu.make_async_copy(w3_hbm.at[layer, e], w3_raw.at[s], sems.at[s, 1]),
                pltpu.make_async_copy(w2_hbm.at[layer, e], w2_raw.at[s], sems.at[s, 2]))

    @pl.when(it_ref[ITEM_NEW + i] == 1)
    def _():
        @pl.when(i == 0)
        def _():
            for cp in weight_copies(expert, slot):
                cp.start(priority=WEIGHT_DMA_PRIORITY)

        for cp in weight_copies(expert, slot):
            cp.wait()

        @pl.when(nxt >= 0)
        def _():
            for cp in weight_copies(nxt, 1 - slot):
                cp.start(priority=WEIGHT_DMA_PRIORITY)

        w1_bf[...] = w1_raw[slot].astype(BF16)
        w3_bf[...] = w3_raw[slot].astype(BF16)
        w2_bf[...] = w2_raw[slot].astype(BF16)

    new_tile = (i == 0) | (tile != prev_tile)

    def tile_work(p):
        cur = xbufs[p]

        @pl.when(new_tile)
        def _():
            @pl.when(i == 0)
            def _():
                gather_rows(tile, p, ROW_DMA_UNROLL)

            @pl.when(tile + 1 < N_SORT_TILES)
            def _():
                gather_rows(tile + 1, 1 - p, ROW_TILE)

            pltpu.make_async_copy(cur, cur, row_sems.at[p]).wait()
            o_ref[...] = jnp.zeros_like(o_ref)

        @pl.when(hi > lo)
        def _():
            xb = cur[...].astype(BF16)
            a = jnp.dot(xb, w1_bf[...], preferred_element_type=F32)
            b = jnp.dot(xb, w3_bf[...], preferred_element_type=F32)
            rows = lax.broadcasted_iota(I32, (ROW_TILE, 1), 0)
            hid = jnp.where((rows >= lo) & (rows < hi), a * _sigmoid(a) * b, 0.0)
            o_ref[...] += jnp.dot(hid.astype(BF16), w2_bf[...], preferred_element_type=F32)

    for p in range(2):
        pl.when(tile & 1 == p)(functools.partial(tile_work, p))


def _gmm(items, src, h_flat, w1, w3, w2, layer):
    hbm = pl.BlockSpec(memory_space=pl.ANY)
    return pl.pallas_call(
        functools.partial(_gmm_kernel, layer=layer),
        grid_spec=pltpu.PrefetchScalarGridSpec(
            num_scalar_prefetch=2,
            grid=(N_ITEMS,),
            in_specs=[hbm, hbm, hbm, hbm],
            out_specs=pl.BlockSpec((ROW_TILE, D), lambda i, it, src: (it[i], 0)),
            scratch_shapes=[pltpu.VMEM((ROW_TILE, D), F32),
                            pltpu.VMEM((ROW_TILE, D), F32),
                            pltpu.VMEM((2, D, EXPERT_FF), F32),
                            pltpu.VMEM((2, D, EXPERT_FF), F32),
                            pltpu.VMEM((2, EXPERT_FF, D), F32),
                            pltpu.VMEM((D, EXPERT_FF), BF16),
                            pltpu.VMEM((D, EXPERT_FF), BF16),
                            pltpu.VMEM((EXPERT_FF, D), BF16),
                            pltpu.SemaphoreType.DMA((2, 3)),
                            pltpu.SemaphoreType.DMA((2,))],
        ),
        out_shape=jax.ShapeDtypeStruct((NSLOT, D), F32),
        compiler_params=pltpu.CompilerParams(
            dimension_semantics=("arbitrary",), vmem_limit_bytes=VMEM_LIMIT),
        name="moe_experts",
    )(items, src, h_flat, w1, w3, w2)


def _combine_kernel(pos_ref, ys_hbm, x_ref, ri_ref, modb_ref, modc_ref, fg_ref, o_ref,
                    buf_a, buf_b, gate_s, sems, *, final, tiles_per_b, m_off):
    i = pl.program_id(0)
    ntiles = pl.num_programs(0)
    bufs = (buf_a, buf_b)

    def issue(tile_i, p, unroll):
        t0 = (tile_i // tiles_per_b) * T + (tile_i % tiles_per_b + m_off) * ROW_TILE

        def rows(g, carry):
            for k in range(unroll):
                r = g * unroll + k
                for s in range(2):
                    pltpu.make_async_copy(ys_hbm.at[pl.ds(pos_ref[s * NTOK + t0 + r], 1)],
                                          bufs[p].at[s, pl.ds(r, 1)], sems.at[p]).start(priority=s)
            return carry

        if unroll == ROW_TILE:
            rows(0, 0)
        else:
            lax.fori_loop(0, ROW_TILE // unroll, rows, 0)

    @pl.when(i == 0)
    def _():
        issue(0, 0, ROW_DMA_UNROLL)

    m = i % tiles_per_b + m_off
    gate_s[...] = jnp.where(m == 0, modc_ref[5:6, :], modb_ref[5:6, :])

    def tile_work(p):
        cur = bufs[p]
        more = i + 1 < ntiles

        if not final:
            @pl.when(more)
            def _():
                issue(i + 1, 1 - p, ROW_TILE)

        pltpu.make_async_copy(cur, cur, sems.at[p]).wait()

        def gated_residual():
            for c in range(ROW_TILE // NM_CHUNK):
                rows = pl.ds(c * NM_CHUNK, NM_CHUNK)
                g0 = ri_ref[rows, 2:3]
                g1 = ri_ref[rows, 3:4]
                xn = x_ref[rows, :] + gate_s[...] * (g0 * cur[0, rows, :] + g1 * cur[1, rows, :])
                if final:
                    inv = lax.rsqrt(jnp.mean(xn * xn, axis=-1, keepdims=True) + RMS_EPS)
                    xn = xn * inv * fg_ref[...]
                o_ref[rows, :] = xn

        if final:
            @pl.when(more)
            def _():
                issue(i + 1, 1 - p, ROW_TILE)
                gated_residual()

            @pl.when(jnp.logical_not(more))
            def _():
                gated_residual()
        else:
            gated_residual()

    for p in range(2):
        pl.when(i % 2 == p)(functools.partial(tile_work, p))


def _combine(pos_flat, ys, xc, ri, mods, layer, final_g, final):
    if final:
        tiles_per_b, m_off, rows_out = SEQ // ROW_TILE, CTX // ROW_TILE, SEQ
    else:
        tiles_per_b, m_off, rows_out = T // ROW_TILE, 0, T
    tok = lambda width: pl.BlockSpec(
        (None, ROW_TILE, width), lambda i, pos: (i // tiles_per_b, i % tiles_per_b + m_off, 0))
    return pl.pallas_call(
        functools.partial(_combine_kernel, final=final, tiles_per_b=tiles_per_b, m_off=m_off),
        grid_spec=pltpu.PrefetchScalarGridSpec(
            num_scalar_prefetch=1,
            grid=(B * tiles_per_b,),
            in_specs=[
                pl.BlockSpec(memory_space=pl.ANY),
                tok(D),
                tok(ROUTE_LANES),
                pl.BlockSpec((None, None, 6, D), lambda i, pos: (layer, i // tiles_per_b, 0, 0)),
                pl.BlockSpec((None, None, 6, D), lambda i, pos: (layer, B, 0, 0)),
                pl.BlockSpec((1, D), lambda i, pos: (0, 0)),
            ],
            out_specs=pl.BlockSpec((None, ROW_TILE, D),
                                   lambda i, pos: (i // tiles_per_b, i % tiles_per_b, 0)),
            scratch_shapes=[pltpu.VMEM((2, ROW_TILE, D), F32),
                            pltpu.VMEM((2, ROW_TILE, D), F32),
                            pltpu.VMEM((1, D), F32),
                            pltpu.SemaphoreType.DMA((2,))],
        ),
        out_shape=jax.ShapeDtypeStruct((B, rows_out, D), F32),
        compiler_params=pltpu.CompilerParams(
            dimension_semantics=("arbitrary",), vmem_limit_bytes=VMEM_LIMIT),
        name="moe_combine",
    )(pos_flat, ys, xc, ri, mods, mods, final_g)


def kernel(x, c, ctx, c_ctx, ada_w, ada_b, norm1_g, norm2_g, ev_w_in, ev_conv_a, ev_dw_w, ev_dw_b,
           ev_ln_g, ev_ln_b, ev_w_out, od_w_in, od_rpb, od_pool_w, od_pool_scale, od_w_out,
           moe_rg_w, moe_rg_b, moe_re_w, moe_re_b, moe_w1, moe_w3, moe_w2, final_g):
    xc = jnp.concatenate([ctx, x], axis=1)
    c8 = jnp.concatenate([c, c_ctx[None, :], jnp.zeros((8 - B - 1, D), F32)], axis=0)
    mods = _ada(c8, ada_w, ada_b).reshape(DEPTH, 8, 6, D)

    ev_w_in_bf = ev_w_in.astype(BF16)
    od_w_in_bf = od_w_in.astype(BF16)
    ev_w_out_bf = ev_w_out.astype(BF16)
    od_w_out_bf = od_w_out.astype(BF16)
    pad = ROUTE_LANES - N_GROUPS - N_EXPERTS
    rw = jnp.concatenate([moe_rg_w, moe_re_w, jnp.zeros((DEPTH, D, pad), F32)], axis=-1)
    rw_hi = rw.astype(BF16)
    rw_lo = (rw - rw_hi.astype(F32)).astype(BF16)
    rb = jnp.concatenate([moe_rg_b, moe_re_b, jnp.zeros((DEPTH, pad), F32)],
                         axis=-1).reshape(DEPTH, 1, ROUTE_LANES)
    final_g2 = final_g.reshape(1, D)
    norm1_g = norm1_g.reshape(DEPTH, 1, D)
    norm2_g = norm2_g.reshape(DEPTH, 1, D)
    ev_ln_g = ev_ln_g.reshape(-1, 1, HALF)
    ev_ln_b = ev_ln_b.reshape(-1, 1, HALF)

    for i in range(DEPTH):
        j = i // 2
        last = i == DEPTH - 1
        if i % 2 == 1:
            u = _norm_mm(xc, mods, i, norm1_g, od_w_in_bf, j)
            ya = _attention(u, od_rpb.reshape(-1), j)
            yb = _pool(u, od_pool_w, od_pool_scale, j)
            xc, hp, ri = _proj(ya, yb, od_w_out_bf, j, ev_ln_g, ev_ln_b, xc, mods, i, norm2_g,
                               rw_hi, rw_lo, rb, even=False)
        else:
            u = _norm_mm(xc, mods, i, norm1_g, ev_w_in_bf, j)
            ya, yb = _conv(u, ev_conv_a, ev_dw_w, ev_dw_b, j)
            xc, hp, ri = _proj(ya, yb, ev_w_out_bf, j, ev_ln_g, ev_ln_b, xc, mods, i, norm2_g,
                               rw_hi, rw_lo, rb, even=True)
        pos, base = _positions(ri.reshape(NTOK, ROUTE_LANES))
        pos_flat = pos.reshape(2 * NTOK)
        items, src = _items(base[0, :N_EXPERTS + 1 + 7], pos_flat)
        ys = _gmm(items, src, hp.reshape(NTOK, D), moe_w1, moe_w3, moe_w2, i)
        xc = _combine(pos_flat, ys, xc, ri, mods, i, final_g2, last)
    return xc
```
